```python
import jax, jax.numpy as jnp
from jax import lax
import numpy as np

D_MODEL = 1024
BATCH = 4
SEQ = 8192
DEPTH = 1

MEM_LEN = 256
MLA_HEADS = 8
MLA_NOPE_DIM = 64
MLA_ROPE_DIM = 32
MLA_V_DIM = 64
MLA_Q_LORA = 384
MLA_KV_LORA = 256
FOX_HEADS = 8
FOX_HEAD_DIM = 64
MEM_HEADS = 4
MEM_HEAD_DIM = 128
N_BRANCHES = 3
D_FF = 2816
BLOCK_Q = 128
ROPE_THETA = 10000.0
LN_EPS = 1e-5
RMS_EPS = 1e-6
DEEPNORM_ALPHA = (2.0 * DEPTH) ** 0.25
DEEPNORM_BETA = (8.0 * DEPTH) ** -0.25
MLA_W = MLA_HEADS * MLA_V_DIM
FOX_W = FOX_HEADS * FOX_HEAD_DIM
MEM_W = MEM_HEADS * MEM_HEAD_DIM
IN_COLS = MLA_Q_LORA + MLA_KV_LORA + MLA_ROPE_DIM + 3 * FOX_W + FOX_HEADS + MEM_W + N_BRANCHES * D_MODEL

kernel_name = "hybrid_mla_fox_memxattn_macaron_deepnorm"


def layer_norm(x, g, b):
    xf = x.astype(jnp.float32)
    mu = jnp.mean(xf, axis=-1, keepdims=True)
    var = jnp.mean(jnp.square(xf - mu), axis=-1, keepdims=True)
    y = (xf - mu) * lax.rsqrt(var + LN_EPS)
    return (y * g.astype(jnp.float32) + b.astype(jnp.float32)).astype(x.dtype)


def rms_norm(x, g):
    xf = x.astype(jnp.float32)
    y = xf * lax.rsqrt(jnp.mean(jnp.square(xf), axis=-1, keepdims=True) + RMS_EPS)
    return (y * g.astype(jnp.float32)).astype(x.dtype)


def apply_rope(x, positions):
    half = x.shape[-1] // 2
    inv_freq = ROPE_THETA ** (-jnp.arange(half, dtype=jnp.float32) / half)
    ang = positions.astype(jnp.float32)[..., None] * inv_freq
    cos = jnp.cos(ang)[:, :, None, :].astype(x.dtype)
    sin = jnp.sin(ang)[:, :, None, :].astype(x.dtype)
    x1, x2 = x[..., :half], x[..., half:]
    return jnp.concatenate([x1 * cos - x2 * sin, x2 * cos + x1 * sin], axis=-1)


def swiglu_ffn(x, w_in, w_down):
    a, b = jnp.split(x @ w_in, 2, axis=-1)
    return (jax.nn.silu(a) * b) @ w_down


def causal_block_attention(q, k, v, scale, fcum=None):
    b, h, s, dk = q.shape
    dv = v.shape[-1]
    nb = s // BLOCK_Q
    q_blocks = q.reshape(b, h, nb, BLOCK_Q, dk).transpose(2, 0, 1, 3, 4)
    key_pos = jnp.arange(s)
    xs = (jnp.arange(nb), q_blocks)
    if fcum is not None:
        xs = xs + (fcum.reshape(b, h, nb, BLOCK_Q).transpose(2, 0, 1, 3),)

    def attend_block(blk):
        i, q_i = blk[0], blk[1]
        logits = jnp.einsum('bhqd,bhkd->bhqk', q_i, k, preferred_element_type=jnp.float32) * scale
        if fcum is not None:
            logits = logits + (blk[2][..., :, None] - fcum[..., None, :])
        q_pos = i * BLOCK_Q + jnp.arange(BLOCK_Q)
        causal = key_pos[None, :] <= q_pos[:, None]
        logits = jnp.where(causal, logits, -jnp.inf)
        p = jax.nn.softmax(logits, axis=-1)
        return jnp.einsum('bhqk,bhkd->bhqd', p.astype(v.dtype), v)

    out = lax.map(attend_block, xs)
    return out.transpose(1, 2, 0, 3, 4).reshape(b, h, s, dv)


def hybrid_mixer(h, mem, positions, w_in, b_gate, mla_q_norm, mla_w_uq, mla_kv_norm, mla_w_ukv,
                 fox_b_f, mem_w_kv, w_br_mla, w_br_fox, w_br_mem, w_out):
    B, S, _ = h.shape
    sizes = (MLA_Q_LORA, MLA_KV_LORA, MLA_ROPE_DIM, FOX_W, FOX_W, FOX_W, FOX_HEADS, MEM_W)
    splits = np.cumsum(sizes).tolist()
    proj = h @ w_in
    c_q, c_kv, k_pe, fq, fk, fv, f_logit, mq, gates = jnp.split(proj, splits, axis=-1)

    q = (rms_norm(c_q, mla_q_norm) @ mla_w_uq).reshape(B, S, MLA_HEADS, MLA_NOPE_DIM + MLA_ROPE_DIM)
    q = jnp.concatenate([q[..., :MLA_NOPE_DIM], apply_rope(q[..., MLA_NOPE_DIM:], positions)], axis=-1)
    kv = (rms_norm(c_kv, mla_kv_norm) @ mla_w_ukv).reshape(B, S, MLA_HEADS, MLA_NOPE_DIM + MLA_V_DIM)
    k_nope, v_mla = kv[..., :MLA_NOPE_DIM], kv[..., MLA_NOPE_DIM:]
    k_rope = apply_rope(k_pe[:, :, None, :], positions)
    k = jnp.concatenate([k_nope, jnp.broadcast_to(k_rope, (B, S, MLA_HEADS, MLA_ROPE_DIM))], axis=-1)
    o_mla = causal_block_attention(q.transpose(0, 2, 1, 3), k.transpose(0, 2, 1, 3),
                                   v_mla.transpose(0, 2, 1, 3),
                                   (MLA_NOPE_DIM + MLA_ROPE_DIM) ** -0.5)
    o_mla = o_mla.transpose(0, 2, 1, 3).reshape(B, S, MLA_W)

    log_f = jax.nn.log_sigmoid((f_logit + fox_b_f).astype(jnp.float32))
    fcum = jnp.cumsum(log_f, axis=1).transpose(0, 2, 1)
    to_heads = lambda t: t.reshape(B, S, FOX_HEADS, FOX_HEAD_DIM).transpose(0, 2, 1, 3)
    o_fox = causal_block_attention(to_heads(fq), to_heads(fk), to_heads(fv),
                                   FOX_HEAD_DIM ** -0.5, fcum)
    o_fox = o_fox.transpose(0, 2, 1, 3).reshape(B, S, FOX_W)

    mk, mv = jnp.split(mem @ mem_w_kv, 2, axis=-1)
    mk = mk.reshape(B, MEM_LEN, MEM_HEADS, MEM_HEAD_DIM)
    mv = mv.reshape(B, MEM_LEN, MEM_HEADS, MEM_HEAD_DIM)
    mqh = mq.reshape(B, S, MEM_HEADS, MEM_HEAD_DIM)
    m_logits = jnp.einsum('bshd,bmhd->bhsm', mqh, mk, preferred_element_type=jnp.float32) * MEM_HEAD_DIM ** -0.5
    m_p = jax.nn.softmax(m_logits, axis=-1).astype(mv.dtype)
    o_mem = jnp.einsum('bhsm,bmhd->bshd', m_p, mv).reshape(B, S, MEM_W)

    g = jax.nn.sigmoid(gates + b_gate).reshape(B, S, N_BRANCHES, D_MODEL)
    merged = (g[:, :, 0] * (o_mla @ w_br_mla)
              + g[:, :, 1] * (o_fox @ w_br_fox)
              + g[:, :, 2] * (o_mem @ w_br_mem))
    return merged @ w_out


def setup_inputs(seed: int = 0) -> dict:
    key = jax.random.key(seed)
    ks = jax.random.split(key, 32)
    L = DEPTH

    def nrm(k, shape, scale):
        return jax.random.normal(k, shape, jnp.float32) * scale

    x = nrm(ks[0], (BATCH, SEQ, D_MODEL), 1.0)
    mem = nrm(ks[1], (BATCH, MEM_LEN, D_MODEL), 1.0)
    start = jax.random.randint(ks[2], (BATCH, 1), 0, 1024, dtype=jnp.int32)
    positions = (start + jnp.arange(SEQ, dtype=jnp.int32)[None, :]).astype(jnp.int32)
    return {
        "x": x,
        "mem": mem,
        "positions": positions,
        "ln1_g": 1.0 + nrm(ks[3], (L, D_MODEL), 0.02),
        "ln1_b": nrm(ks[4], (L, D_MODEL), 0.02),
        "ffn1_w_in": nrm(ks[5], (L, D_MODEL, 2 * D_FF), D_MODEL ** -0.5),
        "ffn1_w_down": nrm(ks[6], (L, D_FF, D_MODEL), D_FF ** -0.5 * DEEPNORM_BETA),
        "w_in": nrm(ks[7], (L, D_MODEL, IN_COLS), D_MODEL ** -0.5),
        "b_gate": nrm(ks[8], (L, N_BRANCHES * D_MODEL), 0.01),
        "mla_q_norm": 1.0 + nrm(ks[9], (L, MLA_Q_LORA), 0.02),
        "mla_w_uq": nrm(ks[10], (L, MLA_Q_LORA, MLA_HEADS * (MLA_NOPE_DIM + MLA_ROPE_DIM)), MLA_Q_LORA ** -0.5),
        "mla_kv_norm": 1.0 + nrm(ks[11], (L, MLA_KV_LORA), 0.02),
        "mla_w_ukv": nrm(ks[12], (L, MLA_KV_LORA, MLA_HEADS * (MLA_NOPE_DIM + MLA_V_DIM)), MLA_KV_LORA ** -0.5),
        "fox_b_f": 2.0 + nrm(ks[13], (L, FOX_HEADS), 0.5),
        "mem_w_kv": nrm(ks[14], (L, D_MODEL, 2 * MEM_W), D_MODEL ** -0.5),
        "w_br_mla": nrm(ks[15], (L, MLA_W, D_MODEL), MLA_W ** -0.5),
        "w_br_fox": nrm(ks[16], (L, FOX_W, D_MODEL), FOX_W ** -0.5),
        "w_br_mem": nrm(ks[17], (L, MEM_W, D_MODEL), MEM_W ** -0.5),
        "w_out": nrm(ks[18], (L, D_MODEL, D_MODEL), D_MODEL ** -0.5 * DEEPNORM_BETA),
        "ln2_g": 1.0 + nrm(ks[19], (L, D_MODEL), 0.02),
        "ln2_b": nrm(ks[20], (L, D_MODEL), 0.02),
        "ffn2_w_in": nrm(ks[21], (L, D_MODEL, 2 * D_FF), D_MODEL ** -0.5),
        "ffn2_w_down": nrm(ks[22], (L, D_FF, D_MODEL), D_FF ** -0.5 * DEEPNORM_BETA),
        "ln3_g": 1.0 + nrm(ks[23], (L, D_MODEL), 0.02),
        "ln3_b": nrm(ks[24], (L, D_MODEL), 0.02),
    }


def reference(x, mem, positions, ln1_g, ln1_b, ffn1_w_in, ffn1_w_down, w_in, b_gate,
              mla_q_norm, mla_w_uq, mla_kv_norm, mla_w_ukv, fox_b_f, mem_w_kv,
              w_br_mla, w_br_fox, w_br_mem, w_out, ln2_g, ln2_b,
              ffn2_w_in, ffn2_w_down, ln3_g, ln3_b):
    for l in range(DEPTH):
        x = layer_norm(DEEPNORM_ALPHA * x + 0.5 * swiglu_ffn(x, ffn1_w_in[l], ffn1_w_down[l]),
                       ln1_g[l], ln1_b[l])
        mix = hybrid_mixer(x, mem, positions, w_in[l], b_gate[l], mla_q_norm[l], mla_w_uq[l],
                           mla_kv_norm[l], mla_w_ukv[l], fox_b_f[l], mem_w_kv[l],
                           w_br_mla[l], w_br_fox[l], w_br_mem[l], w_out[l])
        x = layer_norm(DEEPNORM_ALPHA * x + mix, ln2_g[l], ln2_b[l])
        x = layer_norm(DEEPNORM_ALPHA * x + 0.5 * swiglu_ffn(x, ffn2_w_in[l], ffn2_w_down[l]),
                       ln3_g[l], ln3_b[l])
    return x
```

```python
import functools

import numpy as np
import jax
import jax.numpy as jnp
from jax import lax
from jax.experimental import pallas as pl
from jax.experimental.pallas import tpu as pltpu

F32 = jnp.float32
BF16 = jnp.bfloat16

D_MODEL = 1024
MEM_LEN = 256
MLA_HEADS = 8
MLA_NOPE = 64
MLA_ROPE = 32
MLA_V = 64
MLA_Q_LORA = 384
MLA_KV_LORA = 256
FOX_HEADS = 8
FOX_DIM = 64
MEM_HEADS = 4
MEM_DIM = 128
D_FF = 2816
ROPE_THETA = 10000.0
LN_EPS = 1e-5
RMS_EPS = 1e-6
DEPTH = 1
ALPHA = (2.0 * DEPTH) ** 0.25
MLA_W = MLA_HEADS * MLA_V
FOX_W = FOX_HEADS * FOX_DIM
MEM_W = MEM_HEADS * MEM_DIM

HEADS = 8
QK_PAD = 128
V_ROWS = 80
TILE = 512
FF_CHUNKS = 2
VMEM_LIMIT = 56 * 1024 * 1024


def _cparams(sem):
    return pltpu.CompilerParams(dimension_semantics=sem, vmem_limit_bytes=VMEM_LIMIT)


def _nt(a, b):
    return lax.dot_general(a, b, (((1,), (1,)), ((), ())), preferred_element_type=F32)


def _tn(a, b):
    return lax.dot_general(a, b, (((0,), (0,)), ((), ())), preferred_element_type=F32)


def _dot(a, b):
    return jnp.dot(a, b, preferred_element_type=F32)


def _layer_norm(y, g, b):
    mu = jnp.mean(y, axis=-1, keepdims=True)
    d = y - mu
    var = jnp.mean(d * d, axis=-1, keepdims=True)
    return d * lax.rsqrt(var + LN_EPS) * g + b


def _rms_norm(x, g):
    return x * lax.rsqrt(jnp.mean(x * x, axis=-1, keepdims=True) + RMS_EPS) * g


def _split3(x):
    hi = x.astype(BF16).astype(F32)
    r = x - hi
    mid = r.astype(BF16).astype(F32)
    lo = (r - mid).astype(BF16).astype(F32)
    return hi, mid, lo


def _ffn_ln_kernel(x_ref, wa_ref, wb_ref, wd_ref, g_ref, b_ref, o_ref, acc_ref):
    c = pl.program_id(1)
    x = x_ref[...]
    xb = x.astype(BF16)
    a = _dot(xb, wa_ref[...])
    b = _dot(xb, wb_ref[...])
    gated = (a * jax.nn.sigmoid(a) * b).astype(BF16)
    part = _dot(gated, wd_ref[...])

    @pl.when(c == 0)
    def _():
        acc_ref[...] = part

    @pl.when(c > 0)
    def _():
        acc_ref[...] += part

    @pl.when(c == FF_CHUNKS - 1)
    def _():
        y = ALPHA * x + 0.5 * acc_ref[...]
        o_ref[...] = _layer_norm(y, g_ref[...], b_ref[...])


def _ffn_ln(x, w_in, w_down, g, b):
    n = x.shape[0]
    tf = D_FF // FF_CHUNKS
    return pl.pallas_call(
        _ffn_ln_kernel,
        grid=(n // TILE, FF_CHUNKS),
        in_specs=[
            pl.BlockSpec((TILE, D_MODEL), lambda i, c: (i, 0)),
            pl.BlockSpec((D_MODEL, tf), lambda i, c: (0, c)),
            pl.BlockSpec((D_MODEL, tf), lambda i, c: (0, FF_CHUNKS + c)),
            pl.BlockSpec((tf, D_MODEL), lambda i, c: (c, 0)),
            pl.BlockSpec((1, D_MODEL), lambda i, c: (0, 0)),
            pl.BlockSpec((1, D_MODEL), lambda i, c: (0, 0)),
        ],
        out_specs=pl.BlockSpec((TILE, D_MODEL), lambda i, c: (i, 0)),
        out_shape=jax.ShapeDtypeStruct((n, D_MODEL), F32),
        scratch_shapes=[pltpu.VMEM((TILE, D_MODEL), F32)],
        compiler_params=_cparams(("parallel", "arbitrary")),
        name="ffn_ln",
    )(x, w_in, w_in, w_down, g, b)


def _rope_kernel(pos_ref, inv_ref, cos_ref, sin_ref):
    ang = pos_ref[...].astype(F32) * inv_ref[...]
    c = jnp.cos(ang)
    s = jnp.sin(ang)
    cos_ref[...] = jnp.concatenate([c, c], axis=0)
    sin_ref[...] = jnp.concatenate([-s, s], axis=0)


def _rope_tables(positions):
    bsz, seq = positions.shape
    half = MLA_ROPE // 2
    inv_freq = (ROPE_THETA ** (-jnp.arange(half, dtype=F32) / half)).reshape(half, 1)
    ts = min(seq, 2048)
    spec = pl.BlockSpec((None, MLA_ROPE, ts), lambda b, s: (b, 0, s))
    return pl.pallas_call(
        _rope_kernel,
        grid=(bsz, seq // ts),
        in_specs=[pl.BlockSpec((None, 1, ts), lambda b, s: (b, 0, s)),
                  pl.BlockSpec((half, 1), lambda b, s: (0, 0))],
        out_specs=[spec, spec],
        out_shape=[jax.ShapeDtypeStruct((bsz, MLA_ROPE, seq), F32)] * 2,
        compiler_params=_cparams(("parallel", "parallel")),
        name="rope",
    )(positions.reshape(bsz, 1, seq), inv_freq)


def _attn_operand_specs(bsz, seq):
    nt = seq // TILE
    qt_spec = pl.BlockSpec((None, HEADS, None, QK_PAD, TILE), lambda b, s: (b, 0, s, 0, 0))
    k_spec = pl.BlockSpec((None, TILE, HEADS * QK_PAD), lambda b, s: (b, s, 0))
    vt_spec = pl.BlockSpec((None, HEADS, None, V_ROWS, TILE), lambda b, s: (b, 0, s, 0, 0))
    shapes = [jax.ShapeDtypeStruct((bsz, HEADS, nt, QK_PAD, TILE), BF16),
              jax.ShapeDtypeStruct((bsz, seq, HEADS * QK_PAD), BF16),
              jax.ShapeDtypeStruct((bsz, HEADS, nt, V_ROWS, TILE), BF16)]
    return [qt_spec, k_spec, vt_spec], shapes


def _store_vt(vt_ref, vt):
    row = lax.broadcasted_iota(jnp.int32, (V_ROWS - 64, TILE), 0)
    tail = jnp.where(row == 0, 1.0, 0.0).astype(BF16)
    for h in range(HEADS):
        vt_ref[h, 0:64, :] = vt[64 * h:64 * h + 64].astype(BF16)
        vt_ref[h, 64:V_ROWS, :] = tail


def _full(shape):
    return pl.BlockSpec(shape, lambda b, s: (0,) * len(shape))


def _mla_prep_kernel(h_ref, wc_ref, qg_ref, kvg_ref, wuq_ref, wuqsw_ref, wkpe_ref, wk_ref, wv_ref,
                     cos_ref, sin_ref, qt_ref, k_ref, vt_ref):
    scale = (MLA_NOPE + MLA_ROPE) ** -0.5
    hb = h_ref[...].astype(BF16)
    c = _dot(hb, wc_ref[...])
    cqn = _rms_norm(c[:, :MLA_Q_LORA], qg_ref[...]).astype(BF16)
    kvn = _rms_norm(c[:, MLA_Q_LORA:], kvg_ref[...]).astype(BF16)
    cos_t = cos_ref[...]
    sin_t = sin_ref[...]

    q_t = _nt(wuq_ref[...], cqn)
    qsw_t = _nt(wuqsw_ref[...], cqn)
    dq = MLA_NOPE + MLA_ROPE
    for h in range(HEADS):
        qt_ref[h, 0:MLA_NOPE, :] = (q_t[dq * h:dq * h + MLA_NOPE] * scale).astype(BF16)
        rope = (q_t[dq * h + MLA_NOPE:dq * (h + 1)] * cos_t
                + qsw_t[MLA_ROPE * h:MLA_ROPE * (h + 1)] * sin_t)
        qt_ref[h, MLA_NOPE:dq, :] = (rope * scale).astype(BF16)
        qt_ref[h, dq:QK_PAD, :] = jnp.zeros((QK_PAD - dq, TILE), BF16)

    kpe_t = _nt(wkpe_ref[...], hb)
    kr_t = kpe_t[0:MLA_ROPE] * cos_t + kpe_t[MLA_ROPE:2 * MLA_ROPE] * sin_t
    blk_t = jnp.concatenate([jnp.zeros((MLA_NOPE, TILE), F32), kr_t,
                             jnp.zeros((QK_PAD - dq, TILE), F32)], axis=0)
    kr = blk_t.T
    k_nope = _dot(kvn, wk_ref[...])
    for h in range(HEADS):
        k_ref[:, QK_PAD * h:QK_PAD * (h + 1)] = (k_nope[:, QK_PAD * h:QK_PAD * (h + 1)] + kr).astype(BF16)

    _store_vt(vt_ref, _nt(wv_ref[...], kvn))


def _mla_prep(h, w, cos_t, sin_t):
    bsz, seq, _ = h.shape
    out_specs, out_shapes = _attn_operand_specs(bsz, seq)
    tab_spec = pl.BlockSpec((None, MLA_ROPE, TILE), lambda b, s: (b, 0, s))
    names = ["wc", "qg", "kvg", "wuq", "wuqsw", "wkpe", "wk", "wv"]
    return pl.pallas_call(
        _mla_prep_kernel,
        grid=(bsz, seq // TILE),
        in_specs=[pl.BlockSpec((None, TILE, D_MODEL), lambda b, s: (b, s, 0))]
        + [_full(w[n].shape) for n in names] + [tab_spec, tab_spec],
        out_specs=out_specs,
        out_shape=out_shapes,
        compiler_params=_cparams(("parallel", "parallel")),
        name="mla_prep",
    )(h, *[w[n] for n in names], cos_t, sin_t)


def _fox_prep_kernel(h_ref, wk_ref, wfl_ref, bfl_ref, wq_ref, wv_ref, tri_ref, pk_ref, pq_ref,
                     qt_ref, k_ref, vt_ref, carry_ref):
    @pl.when(pl.program_id(1) == 0)
    def _():
        carry_ref[...] = jnp.zeros_like(carry_ref)

    hb = h_ref[...].astype(BF16)
    lane = lax.broadcasted_iota(jnp.int32, (TILE, 128), 1)
    z = _dot(hb, wfl_ref[...]) + bfl_ref[...]
    log_f = jnp.minimum(z, 0.0) - jnp.log1p(jnp.exp(-jnp.abs(z)))
    hi, mid, lo = _split3(log_f)
    parts = jnp.where(lane < 8, hi, jnp.where(lane < 16, mid, jnp.where(lane < 24, lo, 0.0)))
    cum = _dot(tri_ref[...], parts.astype(BF16))
    f = cum + pltpu.roll(cum, 120, 1) + pltpu.roll(cum, 112, 1) + carry_ref[...]
    f = jnp.where(lane < 8, f, 0.0)
    carry_ref[...] = f[TILE - 1:TILE, :]

    fhi, fmid, flo = _split3(f)
    p = (fhi + pltpu.roll(fmid, 8, 1) + pltpu.roll(flo, 16, 1)
         + jnp.where(lane == 24, 1.0, 0.0)).astype(BF16)
    k_aug = _dot(hb, wk_ref[...]) + _dot(p, pk_ref[...])
    k_ref[...] = k_aug.astype(BF16)

    aug_t = _nt(pq_ref[...], p)
    q_t = _nt(wq_ref[...], hb)
    scale = FOX_DIM ** -0.5
    for h in range(HEADS):
        qt_ref[h, 0:64, :] = (q_t[64 * h:64 * h + 64] * scale).astype(BF16)
        qt_ref[h, 64:80, :] = aug_t[16 * h:16 * h + 16].astype(BF16)
        qt_ref[h, 80:QK_PAD, :] = jnp.zeros((QK_PAD - 80, TILE), BF16)

    _store_vt(vt_ref, _nt(wv_ref[...], hb))


def _fox_prep(h, w):
    bsz, seq, _ = h.shape
    out_specs, out_shapes = _attn_operand_specs(bsz, seq)
    names = ["wk", "wfl", "bfl", "wq", "wv", "tri", "pk", "pq"]
    return pl.pallas_call(
        _fox_prep_kernel,
        grid=(bsz, seq // TILE),
        in_specs=[pl.BlockSpec((None, TILE, D_MODEL), lambda b, s: (b, s, 0))]
        + [_full(w[n].shape) for n in names],
        out_specs=out_specs,
        out_shape=out_shapes,
        scratch_shapes=[pltpu.VMEM((1, 128), F32)],
        compiler_params=_cparams(("parallel", "arbitrary")),
        name="fox_prep",
    )(h, *[w[n] for n in names])


def _attn_kernel(qt_ref, k_ref, vt_ref, o_ref, *, n_tiles):
    row = lax.broadcasted_iota(jnp.int32, (TILE, TILE), 0)
    col = lax.broadcasted_iota(jnp.int32, (TILE, TILE), 1)

    def kv_step(j, qt, m, acc, diagonal):
        k = k_ref[pl.ds(pl.multiple_of(j * TILE, TILE), TILE), :]
        s = _dot(k, qt)
        if diagonal:
            s = jnp.where(row > col, -jnp.inf, s)
        m_new = jnp.maximum(m, jnp.max(s, axis=0, keepdims=True))
        alpha = jnp.exp(m - m_new)
        p = jnp.exp(s - m_new).astype(BF16)
        return m_new, alpha * acc + _dot(vt_ref[j], p)

    def q_body(qi, carry):
        qt = qt_ref[qi]
        m0 = jnp.full((1, TILE), -jnp.inf, F32)
        acc0 = jnp.zeros((V_ROWS, TILE), F32)
        m, acc = lax.fori_loop(0, qi, lambda j, c: kv_step(j, qt, c[0], c[1], False), (m0, acc0))
        m, acc = kv_step(qi, qt, m, acc, True)
        o_ref[qi] = (acc[0:64] / acc[64:65]).astype(BF16)
        return carry

    lax.fori_loop(0, n_tiles, q_body, 0)


def _attention(qt, k, vt):
    bsz, heads, nt = qt.shape[:3]
    seq = k.shape[1]
    return pl.pallas_call(
        functools.partial(_attn_kernel, n_tiles=nt),
        grid=(bsz, heads),
        in_specs=[pl.BlockSpec((None, None, nt, QK_PAD, TILE), lambda b, h: (b, h, 0, 0, 0)),
                  pl.BlockSpec((None, seq, QK_PAD), lambda b, h: (b, 0, h)),
                  pl.BlockSpec((None, None, nt, V_ROWS, TILE), lambda b, h: (b, h, 0, 0, 0))],
        out_specs=pl.BlockSpec((None, None, nt, 64, TILE), lambda b, h: (b, h, 0, 0, 0)),
        out_shape=jax.ShapeDtypeStruct((bsz, heads, nt, 64, TILE), BF16),
        compiler_params=_cparams(("parallel", "parallel")),
        name="attn",
    )(qt, k, vt)


def _mem_kv_kernel(mem_ref, w_ref, mk_ref, mv_ref):
    kv = _dot(mem_ref[...].astype(BF16), w_ref[...])
    mk_ref[...] = kv[:, :MEM_W].astype(BF16)
    mv_ref[...] = kv[:, MEM_W:].astype(BF16)


def _mem_kv(mem, w):
    bsz = mem.shape[0]
    spec = pl.BlockSpec((None, MEM_LEN, MEM_W), lambda b: (b, 0, 0))
    return pl.pallas_call(
        _mem_kv_kernel,
        grid=(bsz,),
        in_specs=[pl.BlockSpec((None, MEM_LEN, D_MODEL), lambda b: (b, 0, 0)),
                  pl.BlockSpec(w.shape, lambda b: (0, 0))],
        out_specs=[spec, spec],
        out_shape=[jax.ShapeDtypeStruct((bsz, MEM_LEN, MEM_W), BF16)] * 2,
        compiler_params=_cparams(("parallel",)),
        name="mem_kv",
    )(mem, w)


def _mem_attn_kernel(h_ref, wq_ref, mk_ref, mv_ref, o_ref):
    scale = MEM_DIM ** -0.5
    mq = _dot(h_ref[...].astype(BF16), wq_ref[...]).astype(BF16)
    for h in range(MEM_HEADS):
        sl = slice(MEM_DIM * h, MEM_DIM * (h + 1))
        s = _nt(mq[:, sl], mk_ref[:, sl]) * scale
        e = jnp.exp(s - jnp.max(s, axis=-1, keepdims=True))
        o = _dot(e.astype(BF16), mv_ref[:, sl]) / jnp.sum(e, axis=-1, keepdims=True)
        o_ref[:, sl] = o.astype(BF16)


def _mem_attn(h, wq, mk, mv):
    bsz, seq, _ = h.shape
    kv_spec = pl.BlockSpec((None, MEM_LEN, MEM_W), lambda b, s: (b, 0, 0))
    return pl.pallas_call(
        _mem_attn_kernel,
        grid=(bsz, seq // TILE),
        in_specs=[pl.BlockSpec((None, TILE, D_MODEL), lambda b, s: (b, s, 0)),
                  _full(wq.shape), kv_spec, kv_spec],
        out_specs=pl.BlockSpec((None, TILE, MEM_W), lambda b, s: (b, s, 0)),
        out_shape=jax.ShapeDtypeStruct((bsz, seq, MEM_W), BF16),
        compiler_params=_cparams(("parallel", "parallel")),
        name="mem_attn",
    )(h, wq, mk, mv)


def _merge_kernel(h_ref, omla_ref, ofox_ref, omem_ref, wg_ref, bg_ref, wbm_ref, wbf_ref, wbc_ref,
                  wo_ref, g_ref, b_ref, o_ref):
    h = h_ref[...]
    gates = jax.nn.sigmoid(_dot(h.astype(BF16), wg_ref[...]) + bg_ref[...])
    a_mla = _tn(omla_ref[...].reshape(MLA_W, TILE), wbm_ref[...])
    a_fox = _tn(ofox_ref[...].reshape(FOX_W, TILE), wbf_ref[...])
    a_mem = _dot(omem_ref[...], wbc_ref[...])
    merged = (gates[:, :D_MODEL] * a_mla + gates[:, D_MODEL:2 * D_MODEL] * a_fox
              + gates[:, 2 * D_MODEL:] * a_mem)
    mix = _dot(merged.astype(BF16), wo_ref[...])
    o_ref[...] = _layer_norm(ALPHA * h + mix, g_ref[...], b_ref[...])


def _merge(h, o_mla, o_fox, o_mem, w):
    bsz, seq, _ = h.shape
    ot_spec = pl.BlockSpec((None, HEADS, None, 64, TILE), lambda b, s: (b, 0, s, 0, 0))
    names = ["wg", "bg", "wbm", "wbf", "wbc", "wo", "g", "b"]
    return pl.pallas_call(
        _merge_kernel,
        grid=(bsz, seq // TILE),
        in_specs=[pl.BlockSpec((None, TILE, D_MODEL), lambda b, s: (b, s, 0)), ot_spec, ot_spec,
                  pl.BlockSpec((None, TILE, MEM_W), lambda b, s: (b, s, 0))]
        + [_full(w[n].shape) for n in names],
        out_specs=pl.BlockSpec((None, TILE, D_MODEL), lambda b, s: (b, s, 0)),
        out_shape=jax.ShapeDtypeStruct((bsz, seq, D_MODEL), F32),
        compiler_params=_cparams(("parallel", "parallel")),
        name="merge",
    )(h, o_mla, o_fox, o_mem, *[w[n] for n in names])


def _placement_matrices():
    pk = np.zeros((128, HEADS * QK_PAD), np.float32)
    pq = np.zeros((HEADS * 16, 128), np.float32)
    for h in range(HEADS):
        for part in range(3):
            pk[8 * part + h, QK_PAD * h + 64 + part] = -1.0
            pk[24, QK_PAD * h + 67 + part] = 1.0
            pq[16 * h + part, 24] = 1.0
            pq[16 * h + 3 + part, 8 * part + h] = 1.0
    tri = np.tril(np.ones((TILE, TILE), np.float32))
    return jnp.asarray(pk, BF16), jnp.asarray(pq, BF16), jnp.asarray(tri, BF16)


def _pad_heads(w, width):
    k = w.shape[0]
    w = w.reshape(k, HEADS, width)
    return jnp.pad(w, ((0, 0), (0, 0), (0, QK_PAD - width))).reshape(k, HEADS * QK_PAD)


def _swap_halves(w, axis):
    a, b = jnp.split(w, 2, axis=axis)
    return jnp.concatenate([b, a], axis=axis)


def kernel(x, mem, positions, ln1_g, ln1_b, ffn1_w_in, ffn1_w_down, w_in, b_gate, mla_q_norm, mla_w_uq, mla_kv_norm, mla_w_ukv, fox_b_f, mem_w_kv, w_br_mla, w_br_fox, w_br_mem, w_out, ln2_g, ln2_b, ffn2_w_in, ffn2_w_down, ln3_g, ln3_b):
    bsz, seq, _ = x.shape
    n = bsz * seq
    row = lambda v: v.reshape(1, -1).astype(F32)
    bf = lambda v: v.astype(BF16)

    sizes = (MLA_Q_LORA, MLA_KV_LORA, MLA_ROPE, FOX_W, FOX_W, FOX_W, FOX_HEADS, MEM_W)
    offs = np.cumsum((0,) + sizes).tolist()
    wi = w_in[0]
    w_cq, w_ckv, w_kpe, w_fq, w_fk, w_fv, w_fl, w_mq = (wi[:, offs[i]:offs[i + 1]] for i in range(8))
    w_gates = wi[:, offs[8]:]

    uq = mla_w_uq[0].reshape(MLA_Q_LORA, MLA_HEADS, MLA_NOPE + MLA_ROPE)
    uq_rope_sw = _swap_halves(uq[:, :, MLA_NOPE:], axis=2).reshape(MLA_Q_LORA, MLA_HEADS * MLA_ROPE)
    ukv = mla_w_ukv[0].reshape(MLA_KV_LORA, MLA_HEADS, MLA_NOPE + MLA_V)
    w_kpe_t = jnp.concatenate([w_kpe.T, _swap_halves(w_kpe, axis=1).T,
                               jnp.zeros((QK_PAD - 2 * MLA_ROPE, D_MODEL), F32)], axis=0)
    mla_w = {
        "wc": bf(jnp.concatenate([w_cq, w_ckv], axis=1)),
        "qg": row(mla_q_norm[0]), "kvg": row(mla_kv_norm[0]),
        "wuq": bf(mla_w_uq[0].T), "wuqsw": bf(uq_rope_sw.T), "wkpe": bf(w_kpe_t),
        "wk": bf(_pad_heads(ukv[:, :, :MLA_NOPE].reshape(MLA_KV_LORA, -1), MLA_NOPE)),
        "wv": bf(ukv[:, :, MLA_NOPE:].reshape(MLA_KV_LORA, -1).T),
    }

    pk, pq, tri = _placement_matrices()
    w_fl3 = jnp.concatenate([w_fl, w_fl, w_fl, jnp.zeros((D_MODEL, 128 - 3 * FOX_HEADS), F32)], axis=1)
    b_fl3 = jnp.concatenate([fox_b_f[0]] * 3 + [jnp.zeros((128 - 3 * FOX_HEADS,), F32)]).reshape(1, 128)
    fox_w = {
        "wk": bf(_pad_heads(w_fk, FOX_DIM)), "wfl": bf(w_fl3), "bfl": b_fl3,
        "wq": bf(w_fq.T), "wv": bf(w_fv.T), "tri": tri, "pk": pk, "pq": pq,
    }

    merge_w = {
        "wg": bf(w_gates), "bg": row(b_gate[0]), "wbm": bf(w_br_mla[0]), "wbf": bf(w_br_fox[0]),
        "wbc": bf(w_br_mem[0]), "wo": bf(w_out[0]), "g": row(ln2_g[0]), "b": row(ln2_b[0]),
    }

    h1 = _ffn_ln(x.reshape(n, D_MODEL), bf(ffn1_w_in[0]), bf(ffn1_w_down[0]), row(ln1_g[0]), row(ln1_b[0]))
    h1 = h1.reshape(bsz, seq, D_MODEL)

    cos_t, sin_t = _rope_tables(positions)
    o_mla = _attention(*_mla_prep(h1, mla_w, cos_t, sin_t))
    o_fox = _attention(*_fox_prep(h1, fox_w))
    mk, mv = _mem_kv(mem, bf(mem_w_kv[0]))
    o_mem = _mem_attn(h1, bf(w_mq), mk, mv)

    h2 = _merge(h1, o_mla, o_fox, o_mem, merge_w)
    out = _ffn_ln(h2.reshape(n, D_MODEL), bf(ffn2_w_in[0]), bf(ffn2_w_down[0]), row(ln3_g[0]), row(ln3_b[0]))
    return out.reshape(bsz, seq, D_MODEL)
```

```python
import functools

import numpy as np
import jax
import jax.numpy as jnp
from jax import lax
from jax.experimental import pallas as pl
from jax.experimental.pallas import tpu as pltpu

F32 = jnp.float32
BF16 = jnp.bfloat16

D_MODEL = 1024
MEM_LEN = 256
MLA_HEADS = 8
MLA_NOPE = 64
MLA_ROPE = 32
MLA_V = 64
MLA_Q_LORA = 384
MLA_KV_LORA = 256
FOX_HEADS = 8
FOX_DIM = 64
MEM_HEADS = 4
MEM_DIM = 128
D_FF = 2816
ROPE_THETA = 10000.0
LN_EPS = 1e-5
RMS_EPS = 1e-6
DEPTH = 1
ALPHA = (2.0 * DEPTH) ** 0.25
MLA_W = MLA_HEADS * MLA_V
FOX_W = FOX_HEADS * FOX_DIM
MEM_W = MEM_HEADS * MEM_DIM

HEADS = 8
QK_PAD = 128
V_ROWS = 80
TILE = 512
FF_CHUNKS = 2
VMEM_LIMIT = 56 * 1024 * 1024
LOG2E = 1.4426950408889634


def _cparams(sem):
    return pltpu.CompilerParams(dimension_semantics=sem, vmem_limit_bytes=VMEM_LIMIT)


def _nt(a, b):
    return lax.dot_general(a, b, (((1,), (1,)), ((), ())), preferred_element_type=F32)


def _tn(a, b):
    return lax.dot_general(a, b, (((0,), (0,)), ((), ())), preferred_element_type=F32)


def _dot(a, b):
    return jnp.dot(a, b, preferred_element_type=F32)


def _layer_norm(y, g, b):
    mu = jnp.mean(y, axis=-1, keepdims=True)
    d = y - mu
    var = jnp.mean(d * d, axis=-1, keepdims=True)
    return d * lax.rsqrt(var + LN_EPS) * g + b


def _rms_norm(x, g):
    return x * lax.rsqrt(jnp.mean(x * x, axis=-1, keepdims=True) + RMS_EPS) * g


def _split3(x):
    hi = x.astype(BF16).astype(F32)
    r = x - hi
    mid = r.astype(BF16).astype(F32)
    lo = (r - mid).astype(BF16).astype(F32)
    return hi, mid, lo


def _ffn_ln_kernel(x_ref, wa_ref, wb_ref, wd_ref, g_ref, b_ref, o_ref, acc_ref):
    c = pl.program_id(1)
    x = x_ref[...]
    xb = x.astype(BF16)
    a = _dot(xb, wa_ref[...])
    b = _dot(xb, wb_ref[...])
    gated = (a * jax.nn.sigmoid(a) * b).astype(BF16)
    part = _dot(gated, wd_ref[...])

    @pl.when(c == 0)
    def _():
        acc_ref[...] = part

    @pl.when(c > 0)
    def _():
        acc_ref[...] += part

    @pl.when(c == FF_CHUNKS - 1)
    def _():
        y = ALPHA * x + 0.5 * acc_ref[...]
        o_ref[...] = _layer_norm(y, g_ref[...], b_ref[...])


def _ffn_ln(x, w_in, w_down, g, b):
    n = x.shape[0]
    tf = D_FF // FF_CHUNKS
    return pl.pallas_call(
        _ffn_ln_kernel,
        grid=(n // TILE, FF_CHUNKS),
        in_specs=[
            pl.BlockSpec((TILE, D_MODEL), lambda i, c: (i, 0)),
            pl.BlockSpec((D_MODEL, tf), lambda i, c: (0, c)),
            pl.BlockSpec((D_MODEL, tf), lambda i, c: (0, FF_CHUNKS + c)),
            pl.BlockSpec((tf, D_MODEL), lambda i, c: (c, 0)),
            pl.BlockSpec((1, D_MODEL), lambda i, c: (0, 0)),
            pl.BlockSpec((1, D_MODEL), lambda i, c: (0, 0)),
        ],
        out_specs=pl.BlockSpec((TILE, D_MODEL), lambda i, c: (i, 0)),
        out_shape=jax.ShapeDtypeStruct((n, D_MODEL), F32),
        scratch_shapes=[pltpu.VMEM((TILE, D_MODEL), F32)],
        compiler_params=_cparams(("parallel", "arbitrary")),
        name="ffn_ln",
    )(x, w_in, w_in, w_down, g, b)


def _rope_kernel(pos_ref, inv_ref, cos_ref, sin_ref):
    ang = pos_ref[...].astype(F32) * inv_ref[...]
    c = jnp.cos(ang)
    s = jnp.sin(ang)
    cos_ref[...] = jnp.concatenate([c, c], axis=0)
    sin_ref[...] = jnp.concatenate([-s, s], axis=0)


def _rope_tables(positions):
    bsz, seq = positions.shape
    half = MLA_ROPE // 2
    inv_freq = (ROPE_THETA ** (-jnp.arange(half, dtype=F32) / half)).reshape(half, 1)
    ts = min(seq, 2048)
    spec = pl.BlockSpec((None, MLA_ROPE, ts), lambda b, s: (b, 0, s))
    return pl.pallas_call(
        _rope_kernel,
        grid=(bsz, seq // ts),
        in_specs=[pl.BlockSpec((None, 1, ts), lambda b, s: (b, 0, s)),
                  pl.BlockSpec((half, 1), lambda b, s: (0, 0))],
        out_specs=[spec, spec],
        out_shape=[jax.ShapeDtypeStruct((bsz, MLA_ROPE, seq), F32)] * 2,
        compiler_params=_cparams(("parallel", "parallel")),
        name="rope",
    )(positions.reshape(bsz, 1, seq), inv_freq)


def _attn_operand_specs(bsz, seq):
    nt = seq // TILE
    qt_spec = pl.BlockSpec((None, HEADS, None, QK_PAD, TILE), lambda b, s: (b, 0, s, 0, 0))
    k_spec = pl.BlockSpec((None, TILE, HEADS * QK_PAD), lambda b, s: (b, s, 0))
    vt_spec = pl.BlockSpec((None, HEADS, None, V_ROWS, TILE), lambda b, s: (b, 0, s, 0, 0))
    shapes = [jax.ShapeDtypeStruct((bsz, HEADS, nt, QK_PAD, TILE), BF16),
              jax.ShapeDtypeStruct((bsz, seq, HEADS * QK_PAD), BF16),
              jax.ShapeDtypeStruct((bsz, HEADS, nt, V_ROWS, TILE), BF16)]
    return [qt_spec, k_spec, vt_spec], shapes


def _store_vt(vt_ref, vt):
    row = lax.broadcasted_iota(jnp.int32, (V_ROWS - 64, TILE), 0)
    tail = jnp.where(row == 0, 1.0, 0.0).astype(BF16)
    for h in range(HEADS):
        vt_ref[h, 0:64, :] = vt[64 * h:64 * h + 64].astype(BF16)
        vt_ref[h, 64:V_ROWS, :] = tail


def _full(shape):
    return pl.BlockSpec(shape, lambda b, s: (0,) * len(shape))


def _mla_prep_kernel(h_ref, wc_ref, qg_ref, kvg_ref, wuq_ref, wuqsw_ref, wkpe_ref, wk_ref, wv_ref,
                     cos_ref, sin_ref, qt_ref, k_ref, vt_ref):
    scale = (MLA_NOPE + MLA_ROPE) ** -0.5 * LOG2E
    hb = h_ref[...].astype(BF16)
    c = _dot(hb, wc_ref[...])
    cqn = _rms_norm(c[:, :MLA_Q_LORA], qg_ref[...]).astype(BF16)
    kvn = _rms_norm(c[:, MLA_Q_LORA:], kvg_ref[...]).astype(BF16)
    cos_t = cos_ref[...]
    sin_t = sin_ref[...]

    q_t = _nt(wuq_ref[...], cqn)
    qsw_t = _nt(wuqsw_ref[...], cqn)
    dq = MLA_NOPE + MLA_ROPE
    for h in range(HEADS):
        qt_ref[h, 0:MLA_NOPE, :] = (q_t[dq * h:dq * h + MLA_NOPE] * scale).astype(BF16)
        rope = (q_t[dq * h + MLA_NOPE:dq * (h + 1)] * cos_t
                + qsw_t[MLA_ROPE * h:MLA_ROPE * (h + 1)] * sin_t)
        qt_ref[h, MLA_NOPE:dq, :] = (rope * scale).astype(BF16)
        qt_ref[h, dq:QK_PAD, :] = jnp.zeros((QK_PAD - dq, TILE), BF16)

    kpe_t = _nt(wkpe_ref[...], hb)
    kr_t = kpe_t[0:MLA_ROPE] * cos_t + kpe_t[MLA_ROPE:2 * MLA_ROPE] * sin_t
    blk_t = jnp.concatenate([jnp.zeros((MLA_NOPE, TILE), F32), kr_t,
                             jnp.zeros((QK_PAD - dq, TILE), F32)], axis=0)
    kr = blk_t.T
    k_nope = _dot(kvn, wk_ref[...])
    for h in range(HEADS):
        k_ref[:, QK_PAD * h:QK_PAD * (h + 1)] = (k_nope[:, QK_PAD * h:QK_PAD * (h + 1)] + kr).astype(BF16)

    _store_vt(vt_ref, _nt(wv_ref[...], kvn))


def _mla_prep(h, w, cos_t, sin_t):
    bsz, seq, _ = h.shape
    out_specs, out_shapes = _attn_operand_specs(bsz, seq)
    tab_spec = pl.BlockSpec((None, MLA_ROPE, TILE), lambda b, s: (b, 0, s))
    names = ["wc", "qg", "kvg", "wuq", "wuqsw", "wkpe", "wk", "wv"]
    return pl.pallas_call(
        _mla_prep_kernel,
        grid=(bsz, seq // TILE),
        in_specs=[pl.BlockSpec((None, TILE, D_MODEL), lambda b, s: (b, s, 0))]
        + [_full(w[n].shape) for n in names] + [tab_spec, tab_spec],
        out_specs=out_specs,
        out_shape=out_shapes,
        compiler_params=_cparams(("parallel", "parallel")),
        name="mla_prep",
    )(h, *[w[n] for n in names], cos_t, sin_t)


def _fox_prep_kernel(h_ref, wk_ref, wfl_ref, bfl_ref, wq_ref, wv_ref, tri_ref, pk_ref, pq_ref,
                     qt_ref, k_ref, vt_ref, carry_ref):
    @pl.when(pl.program_id(1) == 0)
    def _():
        carry_ref[...] = jnp.zeros_like(carry_ref)

    hb = h_ref[...].astype(BF16)
    lane = lax.broadcasted_iota(jnp.int32, (TILE, 128), 1)
    z = _dot(hb, wfl_ref[...]) + bfl_ref[...]
    log_f = jnp.minimum(z, 0.0) - jnp.log1p(jnp.exp(-jnp.abs(z)))
    hi, mid, lo = _split3(log_f)
    parts = jnp.where(lane < 8, hi, jnp.where(lane < 16, mid, jnp.where(lane < 24, lo, 0.0)))
    cum = _dot(tri_ref[...], parts.astype(BF16))
    f = cum + pltpu.roll(cum, 120, 1) + pltpu.roll(cum, 112, 1) + carry_ref[...]
    f = jnp.where(lane < 8, f, 0.0)
    carry_ref[...] = f[TILE - 1:TILE, :]

    fhi, fmid, flo = _split3(f * LOG2E)
    p = (fhi + pltpu.roll(fmid, 8, 1) + pltpu.roll(flo, 16, 1)
         + jnp.where(lane == 24, 1.0, 0.0)).astype(BF16)
    k_aug = _dot(hb, wk_ref[...]) + _dot(p, pk_ref[...])
    k_ref[...] = k_aug.astype(BF16)

    aug_t = _nt(pq_ref[...], p)
    q_t = _nt(wq_ref[...], hb)
    scale = FOX_DIM ** -0.5 * LOG2E
    for h in range(HEADS):
        qt_ref[h, 0:64, :] = (q_t[64 * h:64 * h + 64] * scale).astype(BF16)
        qt_ref[h, 64:80, :] = aug_t[16 * h:16 * h + 16].astype(BF16)
        qt_ref[h, 80:QK_PAD, :] = jnp.zeros((QK_PAD - 80, TILE), BF16)

    _store_vt(vt_ref, _nt(wv_ref[...], hb))


def _fox_prep(h, w):
    bsz, seq, _ = h.shape
    out_specs, out_shapes = _attn_operand_specs(bsz, seq)
    names = ["wk", "wfl", "bfl", "wq", "wv", "tri", "pk", "pq"]
    return pl.pallas_call(
        _fox_prep_kernel,
        grid=(bsz, seq // TILE),
        in_specs=[pl.BlockSpec((None, TILE, D_MODEL), lambda b, s: (b, s, 0))]
        + [_full(w[n].shape) for n in names],
        out_specs=out_specs,
        out_shape=out_shapes,
        scratch_shapes=[pltpu.VMEM((1, 128), F32)],
        compiler_params=_cparams(("parallel", "arbitrary")),
        name="fox_prep",
    )(h, *[w[n] for n in names])


def _attn_kernel(qt_ref, k_ref, vt_ref, o_ref, s0_ref, s1_ref, cm_ref, m_ref, acc_ref, *, n_tiles):
    t = TILE
    row = lax.broadcasted_iota(jnp.int32, (t, t), 0)
    col = lax.broadcasted_iota(jnp.int32, (t, t), 1)

    def logits(kblk, qtile, half, s_ref, slot):
        k = k_ref[pl.ds(pl.multiple_of(kblk * t, t), t), :]
        s = _dot(k, qt_ref[2 * qtile + half])
        s_ref[:, half * t:(half + 1) * t] = s
        cm_ref[slot, :, half * t:(half + 1) * t] = jnp.max(s, axis=0, keepdims=True)

    def softmax_pv(kblk, half, s_ref, slot, diagonal):
        cols = slice(half * t, (half + 1) * t)
        s = s_ref[:, cols]
        if diagonal:
            s = jnp.where(row > col, -jnp.inf, s)
            cm = jnp.max(s, axis=0, keepdims=True)
        else:
            cm = cm_ref[slot, :, cols]
        m_old = m_ref[:, cols]
        m_new = jnp.maximum(m_old, cm)
        p = jnp.exp2(s - m_new).astype(BF16)
        pv = _dot(vt_ref[kblk], p)
        acc_ref[:, cols] = jnp.exp2(m_old - m_new) * acc_ref[:, cols] + pv
        m_ref[:, cols] = m_new

    def q_body(qi, carry):
        m_ref[...] = jnp.full(m_ref.shape, -jnp.inf, F32)
        acc_ref[...] = jnp.zeros(acc_ref.shape, F32)
        for half in range(2):
            logits(0, qi, half, s0_ref, 0)

        def chunk(j, c):
            for half in range(2):
                logits(2 * j + 1, qi, half, s1_ref, 1)
            for half in range(2):
                softmax_pv(2 * j, half, s0_ref, 0, False)
            for half in range(2):
                logits(2 * j + 2, qi, half, s0_ref, 0)
            for half in range(2):
                softmax_pv(2 * j + 1, half, s1_ref, 1, False)
            return c

        lax.fori_loop(0, qi, chunk, 0)
        logits(2 * qi + 1, qi, 1, s1_ref, 1)
        softmax_pv(2 * qi, 0, s0_ref, 0, True)
        softmax_pv(2 * qi, 1, s0_ref, 0, False)
        softmax_pv(2 * qi + 1, 1, s1_ref, 1, True)
        for half in range(2):
            cols = slice(half * t, (half + 1) * t)
            o_ref[2 * qi + half] = (acc_ref[0:64, cols] / acc_ref[64:65, cols]).astype(BF16)
        return carry

    lax.fori_loop(0, n_tiles // 2, q_body, 0)


def _attention(qt, k, vt):
    bsz, heads, nt = qt.shape[:3]
    seq = k.shape[1]
    return pl.pallas_call(
        functools.partial(_attn_kernel, n_tiles=nt),
        grid=(bsz, heads),
        in_specs=[pl.BlockSpec((None, None, nt, QK_PAD, TILE), lambda b, h: (b, h, 0, 0, 0)),
                  pl.BlockSpec((None, seq, QK_PAD), lambda b, h: (b, 0, h)),
                  pl.BlockSpec((None, None, nt, V_ROWS, TILE), lambda b, h: (b, h, 0, 0, 0))],
        out_specs=pl.BlockSpec((None, None, nt, 64, TILE), lambda b, h: (b, h, 0, 0, 0)),
        out_shape=jax.ShapeDtypeStruct((bsz, heads, nt, 64, TILE), BF16),
        scratch_shapes=[pltpu.VMEM((TILE, 2 * TILE), F32), pltpu.VMEM((TILE, 2 * TILE), F32),
                        pltpu.VMEM((2, 1, 2 * TILE), F32), pltpu.VMEM((1, 2 * TILE), F32),
                        pltpu.VMEM((V_ROWS, 2 * TILE), F32)],
        compiler_params=_cparams(("parallel", "parallel")),
        name="attn",
    )(qt, k, vt)


def _mem_kv_kernel(mem_ref, w_ref, mk_ref, mv_ref):
    kv = _dot(mem_ref[...].astype(BF16), w_ref[...])
    mk_ref[...] = kv[:, :MEM_W].astype(BF16)
    mv_ref[...] = kv[:, MEM_W:].astype(BF16)


def _mem_kv(mem, w):
    bsz = mem.shape[0]
    spec = pl.BlockSpec((None, MEM_LEN, MEM_W), lambda b: (b, 0, 0))
    return pl.pallas_call(
        _mem_kv_kernel,
        grid=(bsz,),
        in_specs=[pl.BlockSpec((None, MEM_LEN, D_MODEL), lambda b: (b, 0, 0)),
                  pl.BlockSpec(w.shape, lambda b: (0, 0))],
        out_specs=[spec, spec],
        out_shape=[jax.ShapeDtypeStruct((bsz, MEM_LEN, MEM_W), BF16)] * 2,
        compiler_params=_cparams(("parallel",)),
        name="mem_kv",
    )(mem, w)


def _mem_attn_kernel(h_ref, wq_ref, mk_ref, mv_ref, o_ref):
    scale = MEM_DIM ** -0.5
    mq = _dot(h_ref[...].astype(BF16), wq_ref[...]).astype(BF16)
    for h in range(MEM_HEADS):
        sl = slice(MEM_DIM * h, MEM_DIM * (h + 1))
        s = _nt(mq[:, sl], mk_ref[:, sl]) * scale
        e = jnp.exp(s - jnp.max(s, axis=-1, keepdims=True))
        o = _dot(e.astype(BF16), mv_ref[:, sl]) / jnp.sum(e, axis=-1, keepdims=True)
        o_ref[:, sl] = o.astype(BF16)


def _mem_attn(h, wq, mk, mv):
    bsz, seq, _ = h.shape
    kv_spec = pl.BlockSpec((None, MEM_LEN, MEM_W), lambda b, s: (b, 0, 0))
    return pl.pallas_call(
        _mem_attn_kernel,
        grid=(bsz, seq // TILE),
        in_specs=[pl.BlockSpec((None, TILE, D_MODEL), lambda b, s: (b, s, 0)),
                  _full(wq.shape), kv_spec, kv_spec],
        out_specs=pl.BlockSpec((None, TILE, MEM_W), lambda b, s: (b, s, 0)),
        out_shape=jax.ShapeDtypeStruct((bsz, seq, MEM_W), BF16),
        compiler_params=_cparams(("parallel", "parallel")),
        name="mem_attn",
    )(h, wq, mk, mv)


def _merge_kernel(h_ref, omla_ref, ofox_ref, omem_ref, wg_ref, bg_ref, wbm_ref, wbf_ref, wbc_ref,
                  wo_ref, g_ref, b_ref, o_ref):
    h = h_ref[...]
    gates = jax.nn.sigmoid(_dot(h.astype(BF16), wg_ref[...]) + bg_ref[...])
    a_mla = _tn(omla_ref[...].reshape(MLA_W, TILE), wbm_ref[...])
    a_fox = _tn(ofox_ref[...].reshape(FOX_W, TILE), wbf_ref[...])
    a_mem = _dot(omem_ref[...], wbc_ref[...])
    merged = (gates[:, :D_MODEL] * a_mla + gates[:, D_MODEL:2 * D_MODEL] * a_fox
              + gates[:, 2 * D_MODEL:] * a_mem)
    mix = _dot(merged.astype(BF16), wo_ref[...])
    o_ref[...] = _layer_norm(ALPHA * h + mix, g_ref[...], b_ref[...])


def _merge(h, o_mla, o_fox, o_mem, w):
    bsz, seq, _ = h.shape
    ot_spec = pl.BlockSpec((None, HEADS, None, 64, TILE), lambda b, s: (b, 0, s, 0, 0))
    names = ["wg", "bg", "wbm", "wbf", "wbc", "wo", "g", "b"]
    return pl.pallas_call(
        _merge_kernel,
        grid=(bsz, seq // TILE),
        in_specs=[pl.BlockSpec((None, TILE, D_MODEL), lambda b, s: (b, s, 0)), ot_spec, ot_spec,
                  pl.BlockSpec((None, TILE, MEM_W), lambda b, s: (b, s, 0))]
        + [_full(w[n].shape) for n in names],
        out_specs=pl.BlockSpec((None, TILE, D_MODEL), lambda b, s: (b, s, 0)),
        out_shape=jax.ShapeDtypeStruct((bsz, seq, D_MODEL), F32),
        compiler_params=_cparams(("parallel", "parallel")),
        name="merge",
    )(h, o_mla, o_fox, o_mem, *[w[n] for n in names])


def _placement_matrices():
    pk = np.zeros((128, HEADS * QK_PAD), np.float32)
    pq = np.zeros((HEADS * 16, 128), np.float32)
    for h in range(HEADS):
        for part in range(3):
            pk[8 * part + h, QK_PAD * h + 64 + part] = -1.0
            pk[24, QK_PAD * h + 67 + part] = 1.0
            pq[16 * h + part, 24] = 1.0
            pq[16 * h + 3 + part, 8 * part + h] = 1.0
    tri = np.tril(np.ones((TILE, TILE), np.float32))
    return jnp.asarray(pk, BF16), jnp.asarray(pq, BF16), jnp.asarray(tri, BF16)


def _pad_heads(w, width):
    k = w.shape[0]
    w = w.reshape(k, HEADS, width)
    return jnp.pad(w, ((0, 0), (0, 0), (0, QK_PAD - width))).reshape(k, HEADS * QK_PAD)


def _swap_halves(w, axis):
    a, b = jnp.split(w, 2, axis=axis)
    return jnp.concatenate([b, a], axis=axis)


def kernel(x, mem, positions, ln1_g, ln1_b, ffn1_w_in, ffn1_w_down, w_in, b_gate, mla_q_norm, mla_w_uq, mla_kv_norm, mla_w_ukv, fox_b_f, mem_w_kv, w_br_mla, w_br_fox, w_br_mem, w_out, ln2_g, ln2_b, ffn2_w_in, ffn2_w_down, ln3_g, ln3_b):
    bsz, seq, _ = x.shape
    n = bsz * seq
    row = lambda v: v.reshape(1, -1).astype(F32)
    bf = lambda v: v.astype(BF16)

    sizes = (MLA_Q_LORA, MLA_KV_LORA, MLA_ROPE, FOX_W, FOX_W, FOX_W, FOX_HEADS, MEM_W)
    offs = np.cumsum((0,) + sizes).tolist()
    wi = w_in[0]
    w_cq, w_ckv, w_kpe, w_fq, w_fk, w_fv, w_fl, w_mq = (wi[:, offs[i]:offs[i + 1]] for i in range(8))
    w_gates = wi[:, offs[8]:]

    uq = mla_w_uq[0].reshape(MLA_Q_LORA, MLA_HEADS, MLA_NOPE + MLA_ROPE)
    uq_rope_sw = _swap_halves(uq[:, :, MLA_NOPE:], axis=2).reshape(MLA_Q_LORA, MLA_HEADS * MLA_ROPE)
    ukv = mla_w_ukv[0].reshape(MLA_KV_LORA, MLA_HEADS, MLA_NOPE + MLA_V)
    w_kpe_t = jnp.concatenate([w_kpe.T, _swap_halves(w_kpe, axis=1).T,
                               jnp.zeros((QK_PAD - 2 * MLA_ROPE, D_MODEL), F32)], axis=0)
    mla_w = {
        "wc": bf(jnp.concatenate([w_cq, w_ckv], axis=1)),
        "qg": row(mla_q_norm[0]), "kvg": row(mla_kv_norm[0]),
        "wuq": bf(mla_w_uq[0].T), "wuqsw": bf(uq_rope_sw.T), "wkpe": bf(w_kpe_t),
        "wk": bf(_pad_heads(ukv[:, :, :MLA_NOPE].reshape(MLA_KV_LORA, -1), MLA_NOPE)),
        "wv": bf(ukv[:, :, MLA_NOPE:].reshape(MLA_KV_LORA, -1).T),
    }

    pk, pq, tri = _placement_matrices()
    w_fl3 = jnp.concatenate([w_fl, w_fl, w_fl, jnp.zeros((D_MODEL, 128 - 3 * FOX_HEADS), F32)], axis=1)
    b_fl3 = jnp.concatenate([fox_b_f[0]] * 3 + [jnp.zeros((128 - 3 * FOX_HEADS,), F32)]).reshape(1, 128)
    fox_w = {
        "wk": bf(_pad_heads(w_fk, FOX_DIM)), "wfl": bf(w_fl3), "bfl": b_fl3,
        "wq": bf(w_fq.T), "wv": bf(w_fv.T), "tri": tri, "pk": pk, "pq": pq,
    }

    merge_w = {
        "wg": bf(w_gates), "bg": row(b_gate[0]), "wbm": bf(w_br_mla[0]), "wbf": bf(w_br_fox[0]),
        "wbc": bf(w_br_mem[0]), "wo": bf(w_out[0]), "g": row(ln2_g[0]), "b": row(ln2_b[0]),
    }

    h1 = _ffn_ln(x.reshape(n, D_MODEL), bf(ffn1_w_in[0]), bf(ffn1_w_down[0]), row(ln1_g[0]), row(ln1_b[0]))
    h1 = h1.reshape(bsz, seq, D_MODEL)

    cos_t, sin_t = _rope_tables(positions)
    o_mla = _attention(*_mla_prep(h1, mla_w, cos_t, sin_t))
    o_fox = _attention(*_fox_prep(h1, fox_w))
    mk, mv = _mem_kv(mem, bf(mem_w_kv[0]))
    o_mem = _mem_attn(h1, bf(w_mq), mk, mv)

    h2 = _merge(h1, o_mla, o_fox, o_mem, merge_w)
    out = _ffn_ln(h2.reshape(n, D_MODEL), bf(ffn2_w_in[0]), bf(ffn2_w_down[0]), row(ln3_g[0]), row(ln3_b[0]))
    return out.reshape(bsz, seq, D_MODEL)
```

```python
import functools

import numpy as np
import jax
import jax.numpy as jnp
from jax import lax
from jax.experimental import pallas as pl
from jax.experimental.pallas import tpu as pltpu

F32 = jnp.float32
BF16 = jnp.bfloat16

D_MODEL = 1024
MEM_LEN = 256
MLA_HEADS = 8
MLA_NOPE = 64
MLA_ROPE = 32
MLA_V = 64
MLA_Q_LORA = 384
MLA_KV_LORA = 256
FOX_HEADS = 8
FOX_DIM = 64
MEM_HEADS = 4
MEM_DIM = 128
D_FF = 2816
ROPE_THETA = 10000.0
LN_EPS = 1e-5
RMS_EPS = 1e-6
DEPTH = 1
ALPHA = (2.0 * DEPTH) ** 0.25
MLA_W = MLA_HEADS * MLA_V
FOX_W = FOX_HEADS * FOX_DIM
MEM_W = MEM_HEADS * MEM_DIM

HEADS = 8
QK_PAD = 128
V_ROWS = 80
TILE = 512
FF_CHUNKS = 2
VMEM_LIMIT = 56 * 1024 * 1024
LOG2E = 1.4426950408889634


def _cparams(sem):
    return pltpu.CompilerParams(dimension_semantics=sem, vmem_limit_bytes=VMEM_LIMIT)


def _nt(a, b):
    return lax.dot_general(a, b, (((1,), (1,)), ((), ())), preferred_element_type=F32)


def _tn(a, b):
    return lax.dot_general(a, b, (((0,), (0,)), ((), ())), preferred_element_type=F32)


def _dot(a, b):
    return jnp.dot(a, b, preferred_element_type=F32)


def _layer_norm(y, g, b):
    mu = jnp.mean(y, axis=-1, keepdims=True)
    d = y - mu
    var = jnp.mean(d * d, axis=-1, keepdims=True)
    return d * lax.rsqrt(var + LN_EPS) * g + b


def _rms_norm(x, g):
    return x * lax.rsqrt(jnp.mean(x * x, axis=-1, keepdims=True) + RMS_EPS) * g


def _split3(x):
    hi = x.astype(BF16).astype(F32)
    r = x - hi
    mid = r.astype(BF16).astype(F32)
    lo = (r - mid).astype(BF16).astype(F32)
    return hi, mid, lo


def _ffn_ln_kernel(x_ref, wa_ref, wb_ref, wd_ref, g_ref, b_ref, o_ref, acc_ref):
    c = pl.program_id(1)
    x = x_ref[...]
    xb = x.astype(BF16)
    a = _dot(xb, wa_ref[...])
    b = _dot(xb, wb_ref[...])
    gated = (a * jax.nn.sigmoid(a) * b).astype(BF16)
    part = _dot(gated, wd_ref[...])

    @pl.when(c == 0)
    def _():
        acc_ref[...] = part

    @pl.when(c > 0)
    def _():
        acc_ref[...] += part

    @pl.when(c == FF_CHUNKS - 1)
    def _():
        y = ALPHA * x + 0.5 * acc_ref[...]
        o_ref[...] = _layer_norm(y, g_ref[...], b_ref[...])


def _ffn_ln(x, w_in, w_down, g, b):
    n = x.shape[0]
    tf = D_FF // FF_CHUNKS
    return pl.pallas_call(
        _ffn_ln_kernel,
        grid=(n // TILE, FF_CHUNKS),
        in_specs=[
            pl.BlockSpec((TILE, D_MODEL), lambda i, c: (i, 0)),
            pl.BlockSpec((D_MODEL, tf), lambda i, c: (0, c)),
            pl.BlockSpec((D_MODEL, tf), lambda i, c: (0, FF_CHUNKS + c)),
            pl.BlockSpec((tf, D_MODEL), lambda i, c: (c, 0)),
            pl.BlockSpec((1, D_MODEL), lambda i, c: (0, 0)),
            pl.BlockSpec((1, D_MODEL), lambda i, c: (0, 0)),
        ],
        out_specs=pl.BlockSpec((TILE, D_MODEL), lambda i, c: (i, 0)),
        out_shape=jax.ShapeDtypeStruct((n, D_MODEL), F32),
        scratch_shapes=[pltpu.VMEM((TILE, D_MODEL), F32)],
        compiler_params=_cparams(("parallel", "arbitrary")),
        name="ffn_ln",
    )(x, w_in, w_in, w_down, g, b)


def _rope_kernel(pos_ref, inv_ref, cos_ref, sin_ref):
    ang = pos_ref[...].astype(F32) * inv_ref[...]
    c = jnp.cos(ang)
    s = jnp.sin(ang)
    cos_ref[...] = jnp.concatenate([c, c], axis=0)
    sin_ref[...] = jnp.concatenate([-s, s], axis=0)


def _rope_tables(positions):
    bsz, seq = positions.shape
    half = MLA_ROPE // 2
    inv_freq = (ROPE_THETA ** (-jnp.arange(half, dtype=F32) / half)).reshape(half, 1)
    ts = min(seq, 2048)
    spec = pl.BlockSpec((None, MLA_ROPE, ts), lambda b, s: (b, 0, s))
    return pl.pallas_call(
        _rope_kernel,
        grid=(bsz, seq // ts),
        in_specs=[pl.BlockSpec((None, 1, ts), lambda b, s: (b, 0, s)),
                  pl.BlockSpec((half, 1), lambda b, s: (0, 0))],
        out_specs=[spec, spec],
        out_shape=[jax.ShapeDtypeStruct((bsz, MLA_ROPE, seq), F32)] * 2,
        compiler_params=_cparams(("parallel", "parallel")),
        name="rope",
    )(positions.reshape(bsz, 1, seq), inv_freq)


def _attn_operand_specs(bsz, seq):
    nt = seq // TILE
    qt_spec = pl.BlockSpec((None, HEADS, None, QK_PAD, TILE), lambda b, s: (b, 0, s, 0, 0))
    k_spec = pl.BlockSpec((None, TILE, HEADS * QK_PAD), lambda b, s: (b, s, 0))
    vt_spec = pl.BlockSpec((None, HEADS, None, V_ROWS, TILE), lambda b, s: (b, 0, s, 0, 0))
    shapes = [jax.ShapeDtypeStruct((bsz, HEADS, nt, QK_PAD, TILE), BF16),
              jax.ShapeDtypeStruct((bsz, seq, HEADS * QK_PAD), BF16),
              jax.ShapeDtypeStruct((bsz, HEADS, nt, V_ROWS, TILE), BF16)]
    return [qt_spec, k_spec, vt_spec], shapes


def _store_vt(vt_ref, vt):
    row = lax.broadcasted_iota(jnp.int32, (V_ROWS - 64, TILE), 0)
    tail = jnp.where(row == 0, 1.0, 0.0).astype(BF16)
    for h in range(HEADS):
        vt_ref[h, 0:64, :] = vt[64 * h:64 * h + 64].astype(BF16)
        vt_ref[h, 64:V_ROWS, :] = tail


def _full(shape):
    return pl.BlockSpec(shape, lambda b, s: (0,) * len(shape))


def _mla_prep_kernel(h_ref, wc_ref, qg_ref, kvg_ref, wuq_ref, wuqsw_ref, wkpe_ref, wk_ref, wv_ref,
                     cos_ref, sin_ref, qt_ref, k_ref, vt_ref):
    scale = (MLA_NOPE + MLA_ROPE) ** -0.5 * LOG2E
    hb = h_ref[...].astype(BF16)
    c = _dot(hb, wc_ref[...])
    cqn = _rms_norm(c[:, :MLA_Q_LORA], qg_ref[...]).astype(BF16)
    kvn = _rms_norm(c[:, MLA_Q_LORA:], kvg_ref[...]).astype(BF16)
    cos_t = cos_ref[...]
    sin_t = sin_ref[...]

    q_t = _nt(wuq_ref[...], cqn)
    qsw_t = _nt(wuqsw_ref[...], cqn)
    dq = MLA_NOPE + MLA_ROPE
    for h in range(HEADS):
        qt_ref[h, 0:MLA_NOPE, :] = (q_t[dq * h:dq * h + MLA_NOPE] * scale).astype(BF16)
        rope = (q_t[dq * h + MLA_NOPE:dq * (h + 1)] * cos_t
                + qsw_t[MLA_ROPE * h:MLA_ROPE * (h + 1)] * sin_t)
        qt_ref[h, MLA_NOPE:dq, :] = (rope * scale).astype(BF16)
        qt_ref[h, dq:QK_PAD, :] = jnp.zeros((QK_PAD - dq, TILE), BF16)

    kpe_t = _nt(wkpe_ref[...], hb)
    kr_t = kpe_t[0:MLA_ROPE] * cos_t + kpe_t[MLA_ROPE:2 * MLA_ROPE] * sin_t
    blk_t = jnp.concatenate([jnp.zeros((MLA_NOPE, TILE), F32), kr_t,
                             jnp.zeros((QK_PAD - dq, TILE), F32)], axis=0)
    kr = blk_t.T
    k_nope = _dot(kvn, wk_ref[...])
    for h in range(HEADS):
        k_ref[:, QK_PAD * h:QK_PAD * (h + 1)] = (k_nope[:, QK_PAD * h:QK_PAD * (h + 1)] + kr).astype(BF16)

    _store_vt(vt_ref, _nt(wv_ref[...], kvn))


def _mla_prep(h, w, cos_t, sin_t):
    bsz, seq, _ = h.shape
    out_specs, out_shapes = _attn_operand_specs(bsz, seq)
    tab_spec = pl.BlockSpec((None, MLA_ROPE, TILE), lambda b, s: (b, 0, s))
    names = ["wc", "qg", "kvg", "wuq", "wuqsw", "wkpe", "wk", "wv"]
    return pl.pallas_call(
        _mla_prep_kernel,
        grid=(bsz, seq // TILE),
        in_specs=[pl.BlockSpec((None, TILE, D_MODEL), lambda b, s: (b, s, 0))]
        + [_full(w[n].shape) for n in names] + [tab_spec, tab_spec],
        out_specs=out_specs,
        out_shape=out_shapes,
        compiler_params=_cparams(("parallel", "parallel")),
        name="mla_prep",
    )(h, *[w[n] for n in names], cos_t, sin_t)


def _fox_prep_kernel(h_ref, wk_ref, wfl_ref, bfl_ref, wq_ref, wv_ref, tri_ref, pk_ref, pq_ref,
                     qt_ref, k_ref, vt_ref, carry_ref):
    @pl.when(pl.program_id(1) == 0)
    def _():
        carry_ref[...] = jnp.zeros_like(carry_ref)

    hb = h_ref[...].astype(BF16)
    lane = lax.broadcasted_iota(jnp.int32, (TILE, 128), 1)
    z = _dot(hb, wfl_ref[...]) + bfl_ref[...]
    log_f = jnp.minimum(z, 0.0) - jnp.log1p(jnp.exp(-jnp.abs(z)))
    hi, mid, lo = _split3(log_f)
    parts = jnp.where(lane < 8, hi, jnp.where(lane < 16, mid, jnp.where(lane < 24, lo, 0.0)))
    cum = _dot(tri_ref[...], parts.astype(BF16))
    f = cum + pltpu.roll(cum, 120, 1) + pltpu.roll(cum, 112, 1) + carry_ref[...]
    f = jnp.where(lane < 8, f, 0.0)
    carry_ref[...] = f[TILE - 1:TILE, :]

    fhi, fmid, flo = _split3(f * LOG2E)
    p = (fhi + pltpu.roll(fmid, 8, 1) + pltpu.roll(flo, 16, 1)
         + jnp.where(lane == 24, 1.0, 0.0)).astype(BF16)
    k_aug = _dot(hb, wk_ref[...]) + _dot(p, pk_ref[...])
    k_ref[...] = k_aug.astype(BF16)

    aug_t = _nt(pq_ref[...], p)
    q_t = _nt(wq_ref[...], hb)
    scale = FOX_DIM ** -0.5 * LOG2E
    for h in range(HEADS):
        qt_ref[h, 0:64, :] = (q_t[64 * h:64 * h + 64] * scale).astype(BF16)
        qt_ref[h, 64:80, :] = aug_t[16 * h:16 * h + 16].astype(BF16)
        qt_ref[h, 80:QK_PAD, :] = jnp.zeros((QK_PAD - 80, TILE), BF16)

    _store_vt(vt_ref, _nt(wv_ref[...], hb))


def _fox_prep(h, w):
    bsz, seq, _ = h.shape
    out_specs, out_shapes = _attn_operand_specs(bsz, seq)
    names = ["wk", "wfl", "bfl", "wq", "wv", "tri", "pk", "pq"]
    return pl.pallas_call(
        _fox_prep_kernel,
        grid=(bsz, seq // TILE),
        in_specs=[pl.BlockSpec((None, TILE, D_MODEL), lambda b, s: (b, s, 0))]
        + [_full(w[n].shape) for n in names],
        out_specs=out_specs,
        out_shape=out_shapes,
        scratch_shapes=[pltpu.VMEM((1, 128), F32)],
        compiler_params=_cparams(("parallel", "arbitrary")),
        name="fox_prep",
    )(h, *[w[n] for n in names])


MAX_LAG = 16.0
MAX_OVERSHOOT = 96.0
FAST_CHUNKS_PER_STEP = 2


def _attn_kernel(qt_ref, k_ref, vt_ref, o_ref, s0_ref, s1_ref, s2_ref, s3_ref, cm_ref, m_ref, acc_ref,
                 flag_ref, *, n_tiles):
    t = TILE
    nq = n_tiles // 2
    row = lax.broadcasted_iota(jnp.int32, (t, t), 0)
    col = lax.broadcasted_iota(jnp.int32, (t, t), 1)

    def logits(kblk, qtile, half, s_ref, slot):
        k = k_ref[pl.ds(pl.multiple_of(kblk * t, t), t), :]
        s = _dot(k, qt_ref[2 * qtile + half])
        s_ref[:, half * t:(half + 1) * t] = s
        cm_ref[slot, :, half * t:(half + 1) * t] = jnp.max(s, axis=0, keepdims=True)

    def softmax_pv(kblk, qtile, half, s_ref, slot, diagonal=False, first=False):
        cols = slice(half * t, (half + 1) * t)
        s = s_ref[:, cols]
        if diagonal:
            s = jnp.where(row > col, -jnp.inf, s)
            cm = jnp.max(s, axis=0, keepdims=True)
        else:
            cm = cm_ref[slot, :, cols]
        m_old = cm if first else m_ref[qtile, :, cols]
        m_new = cm if first else jnp.maximum(m_old, cm)
        p = jnp.exp2(s - m_new).astype(BF16)
        pv = _dot(vt_ref[kblk], p)
        if first:
            acc_ref[qtile, :, cols] = pv
        else:
            acc_ref[qtile, :, cols] = jnp.exp2(m_old - m_new) * acc_ref[qtile, :, cols] + pv
        m_ref[qtile, :, cols] = m_new

    def diag_logits(q, sa_ref, sb_ref, slot):
        logits(2 * q, q, 0, sa_ref, slot)
        logits(2 * q, q, 1, sa_ref, slot)
        logits(2 * q + 1, q, 1, sb_ref, slot + 1)

    def diag_softmax_pv(q, sa_ref, sb_ref, slot):
        softmax_pv(2 * q, q, 0, sa_ref, slot, diagonal=True, first=True)
        softmax_pv(2 * q, q, 1, sa_ref, slot, first=True)
        softmax_pv(2 * q + 1, q, 1, sb_ref, slot + 1, diagonal=True)

    def diagonal_phase():
        diag_logits(0, s0_ref, s1_ref, 0)

        def diag_pair(i, c):
            qa, qb = 2 * i, 2 * i + 1
            diag_logits(qb, s2_ref, s3_ref, 2)
            diag_softmax_pv(qa, s0_ref, s1_ref, 0)
            diag_logits(jnp.minimum(qa + 2, nq - 1), s0_ref, s1_ref, 0)
            diag_softmax_pv(qb, s2_ref, s3_ref, 2)
            return c

        lax.fori_loop(0, nq // 2, diag_pair, 0)

    def advance(q, j):
        wrap = j + 1 >= q
        return jnp.where(wrap, jnp.minimum(q + 1, nq - 1), q), jnp.where(wrap, 0, j + 1)

    n_chunks = nq * (nq - 1) // 2
    first_chunk = (jnp.int32(1), jnp.int32(0))

    def fast_unit(q, j, half):
        cols = slice(half * t, (half + 1) * t)
        k = k_ref[pl.ds(pl.multiple_of(j * 2 * t, 2 * t), 2 * t), :]
        s = _dot(k, qt_ref[2 * q + half])
        cm = jnp.max(s, axis=0, keepdims=True)
        m_old = m_ref[q, :, cols]
        p = jnp.exp2(s - m_old).astype(BF16)
        vt = jnp.concatenate([vt_ref[2 * j], vt_ref[2 * j + 1]], axis=1)
        pv = _dot(vt, p)
        m_new = jnp.maximum(m_old, cm - MAX_LAG)
        acc_ref[q, :, cols] = (acc_ref[q, :, cols] + pv) * jnp.exp2(m_old - m_new)
        m_ref[q, :, cols] = m_new
        flag_ref[:, cols] = jnp.maximum(flag_ref[:, cols], cm - m_old)

    def fast_phase():
        def step(i, c):
            for _ in range(FAST_CHUNKS_PER_STEP):
                for half in range(2):
                    fast_unit(c[0], c[1], half)
                c = advance(*c)
            return c

        c = lax.fori_loop(0, n_chunks // FAST_CHUNKS_PER_STEP, step, first_chunk)
        for _ in range(n_chunks % FAST_CHUNKS_PER_STEP):
            for half in range(2):
                fast_unit(c[0], c[1], half)
            c = advance(*c)

    def safe_chunk(q, j, qn, jn):
        for half in range(2):
            logits(2 * j + 1, q, half, s1_ref, 1)
        for half in range(2):
            softmax_pv(2 * j, q, half, s0_ref, 0)
        for half in range(2):
            logits(2 * jn, qn, half, s0_ref, 0)
        for half in range(2):
            softmax_pv(2 * j + 1, q, half, s1_ref, 1)

    def safe_phase():
        for half in range(2):
            logits(0, 1, half, s0_ref, 0)

        def step(i, c):
            nxt = advance(*c)
            safe_chunk(*c, *nxt)
            return nxt

        lax.fori_loop(0, n_chunks, step, first_chunk)

    def finalize():
        def body(q, c):
            for half in range(2):
                cols = slice(half * t, (half + 1) * t)
                o_ref[2 * q + half] = (acc_ref[q, 0:64, cols] / acc_ref[q, 64:65, cols]).astype(BF16)
            return c

        lax.fori_loop(0, nq, body, 0)

    flag_ref[...] = jnp.full(flag_ref.shape, -jnp.inf, F32)
    diagonal_phase()
    if n_chunks:
        fast_phase()
    finalize()

    @pl.when(jnp.max(flag_ref[...]) > MAX_OVERSHOOT)
    def _():
        diagonal_phase()
        if n_chunks:
            safe_phase()
        finalize()


def _attention(qt, k, vt):
    bsz, heads, nt = qt.shape[:3]
    seq = k.shape[1]
    return pl.pallas_call(
        functools.partial(_attn_kernel, n_tiles=nt),
        grid=(bsz, heads),
        in_specs=[pl.BlockSpec((None, None, nt, QK_PAD, TILE), lambda b, h: (b, h, 0, 0, 0)),
                  pl.BlockSpec((None, seq, QK_PAD), lambda b, h: (b, 0, h)),
                  pl.BlockSpec((None, None, nt, V_ROWS, TILE), lambda b, h: (b, h, 0, 0, 0))],
        out_specs=pl.BlockSpec((None, None, nt, 64, TILE), lambda b, h: (b, h, 0, 0, 0)),
        out_shape=jax.ShapeDtypeStruct((bsz, heads, nt, 64, TILE), BF16),
        scratch_shapes=[pltpu.VMEM((TILE, 2 * TILE), F32)] * 4
        + [pltpu.VMEM((4, 1, 2 * TILE), F32), pltpu.VMEM((nt // 2, 1, 2 * TILE), F32),
           pltpu.VMEM((nt // 2, V_ROWS, 2 * TILE), F32), pltpu.VMEM((1, 2 * TILE), F32)],
        compiler_params=_cparams(("parallel", "parallel")),
        name="attn",
    )(qt, k, vt)


def _mem_kv_kernel(mem_ref, w_ref, mk_ref, mv_ref):
    kv = _dot(mem_ref[...].astype(BF16), w_ref[...])
    mk_ref[...] = kv[:, :MEM_W].astype(BF16)
    mv_ref[...] = kv[:, MEM_W:].astype(BF16)


def _mem_kv(mem, w):
    bsz = mem.shape[0]
    spec = pl.BlockSpec((None, MEM_LEN, MEM_W), lambda b: (b, 0, 0))
    return pl.pallas_call(
        _mem_kv_kernel,
        grid=(bsz,),
        in_specs=[pl.BlockSpec((None, MEM_LEN, D_MODEL), lambda b: (b, 0, 0)),
                  pl.BlockSpec(w.shape, lambda b: (0, 0))],
        out_specs=[spec, spec],
        out_shape=[jax.ShapeDtypeStruct((bsz, MEM_LEN, MEM_W), BF16)] * 2,
        compiler_params=_cparams(("parallel",)),
        name="mem_kv",
    )(mem, w)


def _mem_attn_kernel(h_ref, wq_ref, mk_ref, mv_ref, o_ref):
    scale = MEM_DIM ** -0.5
    mq = _dot(h_ref[...].astype(BF16), wq_ref[...]).astype(BF16)
    for h in range(MEM_HEADS):
        sl = slice(MEM_DIM * h, MEM_DIM * (h + 1))
        s = _nt(mq[:, sl], mk_ref[:, sl]) * scale
        e = jnp.exp(s - jnp.max(s, axis=-1, keepdims=True))
        o = _dot(e.astype(BF16), mv_ref[:, sl]) / jnp.sum(e, axis=-1, keepdims=True)
        o_ref[:, sl] = o.astype(BF16)


def _mem_attn(h, wq, mk, mv):
    bsz, seq, _ = h.shape
    kv_spec = pl.BlockSpec((None, MEM_LEN, MEM_W), lambda b, s: (b, 0, 0))
    return pl.pallas_call(
        _mem_attn_kernel,
        grid=(bsz, seq // TILE),
        in_specs=[pl.BlockSpec((None, TILE, D_MODEL), lambda b, s: (b, s, 0)),
                  _full(wq.shape), kv_spec, kv_spec],
        out_specs=pl.BlockSpec((None, TILE, MEM_W), lambda b, s: (b, s, 0)),
        out_shape=jax.ShapeDtypeStruct((bsz, seq, MEM_W), BF16),
        compiler_params=_cparams(("parallel", "parallel")),
        name="mem_attn",
    )(h, wq, mk, mv)


def _merge_kernel(h_ref, omla_ref, ofox_ref, omem_ref, wg_ref, bg_ref, wbm_ref, wbf_ref, wbc_ref,
                  wo_ref, g_ref, b_ref, o_ref):
    h = h_ref[...]
    gates = jax.nn.sigmoid(_dot(h.astype(BF16), wg_ref[...]) + bg_ref[...])
    a_mla = _tn(omla_ref[...].reshape(MLA_W, TILE), wbm_ref[...])
    a_fox = _tn(ofox_ref[...].reshape(FOX_W, TILE), wbf_ref[...])
    a_mem = _dot(omem_ref[...], wbc_ref[...])
    merged = (gates[:, :D_MODEL] * a_mla + gates[:, D_MODEL:2 * D_MODEL] * a_fox
              + gates[:, 2 * D_MODEL:] * a_mem)
    mix = _dot(merged.astype(BF16), wo_ref[...])
    o_ref[...] = _layer_norm(ALPHA * h + mix, g_ref[...], b_ref[...])


def _merge(h, o_mla, o_fox, o_mem, w):
    bsz, seq, _ = h.shape
    ot_spec = pl.BlockSpec((None, HEADS, None, 64, TILE), lambda b, s: (b, 0, s, 0, 0))
    names = ["wg", "bg", "wbm", "wbf", "wbc", "wo", "g", "b"]
    return pl.pallas_call(
        _merge_kernel,
        grid=(bsz, seq // TILE),
        in_specs=[pl.BlockSpec((None, TILE, D_MODEL), lambda b, s: (b, s, 0)), ot_spec, ot_spec,
                  pl.BlockSpec((None, TILE, MEM_W), lambda b, s: (b, s, 0))]
        + [_full(w[n].shape) for n in names],
        out_specs=pl.BlockSpec((None, TILE, D_MODEL), lambda b, s: (b, s, 0)),
        out_shape=jax.ShapeDtypeStruct((bsz, seq, D_MODEL), F32),
        compiler_params=_cparams(("parallel", "parallel")),
        name="merge",
    )(h, o_mla, o_fox, o_mem, *[w[n] for n in names])


def _placement_matrices():
    pk = np.zeros((128, HEADS * QK_PAD), np.float32)
    pq = np.zeros((HEADS * 16, 128), np.float32)
    for h in range(HEADS):
        for part in range(3):
            pk[8 * part + h, QK_PAD * h + 64 + part] = -1.0
            pk[24, QK_PAD * h + 67 + part] = 1.0
            pq[16 * h + part, 24] = 1.0
            pq[16 * h + 3 + part, 8 * part + h] = 1.0
    tri = np.tril(np.ones((TILE, TILE), np.float32))
    return jnp.asarray(pk, BF16), jnp.asarray(pq, BF16), jnp.asarray(tri, BF16)


def _pad_heads(w, width):
    k = w.shape[0]
    w = w.reshape(k, HEADS, width)
    return jnp.pad(w, ((0, 0), (0, 0), (0, QK_PAD - width))).reshape(k, HEADS * QK_PAD)


def _swap_halves(w, axis):
    a, b = jnp.split(w, 2, axis=axis)
    return jnp.concatenate([b, a], axis=axis)


def kernel(x, mem, positions, ln1_g, ln1_b, ffn1_w_in, ffn1_w_down, w_in, b_gate, mla_q_norm, mla_w_uq, mla_kv_norm, mla_w_ukv, fox_b_f, mem_w_kv, w_br_mla, w_br_fox, w_br_mem, w_out, ln2_g, ln2_b, ffn2_w_in, ffn2_w_down, ln3_g, ln3_b):
    bsz, seq, _ = x.shape
    n = bsz * seq
    row = lambda v: v.reshape(1, -1).astype(F32)
    bf = lambda v: v.astype(BF16)

    sizes = (MLA_Q_LORA, MLA_KV_LORA, MLA_ROPE, FOX_W, FOX_W, FOX_W, FOX_HEADS, MEM_W)
    offs = np.cumsum((0,) + sizes).tolist()
    wi = w_in[0]
    w_cq, w_ckv, w_kpe, w_fq, w_fk, w_fv, w_fl, w_mq = (wi[:, offs[i]:offs[i + 1]] for i in range(8))
    w_gates = wi[:, offs[8]:]

    uq = mla_w_uq[0].reshape(MLA_Q_LORA, MLA_HEADS, MLA_NOPE + MLA_ROPE)
    uq_rope_sw = _swap_halves(uq[:, :, MLA_NOPE:], axis=2).reshape(MLA_Q_LORA, MLA_HEADS * MLA_ROPE)
    ukv = mla_w_ukv[0].reshape(MLA_KV_LORA, MLA_HEADS, MLA_NOPE + MLA_V)
    w_kpe_t = jnp.concatenate([w_kpe.T, _swap_halves(w_kpe, axis=1).T,
                               jnp.zeros((QK_PAD - 2 * MLA_ROPE, D_MODEL), F32)], axis=0)
    mla_w = {
        "wc": bf(jnp.concatenate([w_cq, w_ckv], axis=1)),
        "qg": row(mla_q_norm[0]), "kvg": row(mla_kv_norm[0]),
        "wuq": bf(mla_w_uq[0].T), "wuqsw": bf(uq_rope_sw.T), "wkpe": bf(w_kpe_t),
        "wk": bf(_pad_heads(ukv[:, :, :MLA_NOPE].reshape(MLA_KV_LORA, -1), MLA_NOPE)),
        "wv": bf(ukv[:, :, MLA_NOPE:].reshape(MLA_KV_LORA, -1).T),
    }

    pk, pq, tri = _placement_matrices()
    w_fl3 = jnp.concatenate([w_fl, w_fl, w_fl, jnp.zeros((D_MODEL, 128 - 3 * FOX_HEADS), F32)], axis=1)
    b_fl3 = jnp.concatenate([fox_b_f[0]] * 3 + [jnp.zeros((128 - 3 * FOX_HEADS,), F32)]).reshape(1, 128)
    fox_w = {
        "wk": bf(_pad_heads(w_fk, FOX_DIM)), "wfl": bf(w_fl3), "bfl": b_fl3,
        "wq": bf(w_fq.T), "wv": bf(w_fv.T), "tri": tri, "pk": pk, "pq": pq,
    }

    merge_w = {
        "wg": bf(w_gates), "bg": row(b_gate[0]), "wbm": bf(w_br_mla[0]), "wbf": bf(w_br_fox[0]),
        "wbc": bf(w_br_mem[0]), "wo": bf(w_out[0]), "g": row(ln2_g[0]), "b": row(ln2_b[0]),
    }

    h1 = _ffn_ln(x.reshape(n, D_MODEL), bf(ffn1_w_in[0]), bf(ffn1_w_down[0]), row(ln1_g[0]), row(ln1_b[0]))
    h1 = h1.reshape(bsz, seq, D_MODEL)

    cos_t, sin_t = _rope_tables(positions)
    o_mla = _attention(*_mla_prep(h1, mla_w, cos_t, sin_t))
    o_fox = _attention(*_fox_prep(h1, fox_w))
    mk, mv = _mem_kv(mem, bf(mem_w_kv[0]))
    o_mem = _mem_attn(h1, bf(w_mq), mk, mv)

    h2 = _merge(h1, o_mla, o_fox, o_mem, merge_w)
    out = _ffn_ln(h2.reshape(n, D_MODEL), bf(ffn2_w_in[0]), bf(ffn2_w_down[0]), row(ln3_g[0]), row(ln3_b[0]))
    return out.reshape(bsz, seq, D_MODEL)
```

```python
import functools

import numpy as np
import jax
import jax.numpy as jnp
from jax import lax
from jax.experimental import pallas as pl
from jax.experimental.pallas import tpu as pltpu

F32 = jnp.float32
BF16 = jnp.bfloat16

D_MODEL = 1024
MEM_LEN = 256
MLA_HEADS = 8
MLA_NOPE = 64
MLA_ROPE = 32
MLA_V = 64
MLA_Q_LORA = 384
MLA_KV_LORA = 256
FOX_HEADS = 8
FOX_DIM = 64
MEM_HEADS = 4
MEM_DIM = 128
D_FF = 2816
ROPE_THETA = 10000.0
LN_EPS = 1e-5
RMS_EPS = 1e-6
DEPTH = 1
ALPHA = (2.0 * DEPTH) ** 0.25
MLA_W = MLA_HEADS * MLA_V
FOX_W = FOX_HEADS * FOX_DIM
MEM_W = MEM_HEADS * MEM_DIM

HEADS = 8
QK_PAD = 128
V_ROWS = 80
TILE = 512
FF_ROW_GROUPS = 2
MERGE_ROW_GROUPS = 2
VMEM_LIMIT = 56 * 1024 * 1024
LOG2E = 1.4426950408889634


def _cparams(sem):
    return pltpu.CompilerParams(dimension_semantics=sem, vmem_limit_bytes=VMEM_LIMIT)


def _nt(a, b):
    return lax.dot_general(a, b, (((1,), (1,)), ((), ())), preferred_element_type=F32)


def _tn(a, b):
    return lax.dot_general(a, b, (((0,), (0,)), ((), ())), preferred_element_type=F32)


def _dot(a, b):
    return jnp.dot(a, b, preferred_element_type=F32)


def _layer_norm(y, g, b):
    mu = jnp.mean(y, axis=-1, keepdims=True)
    d = y - mu
    var = jnp.mean(d * d, axis=-1, keepdims=True)
    return d * lax.rsqrt(var + LN_EPS) * g + b


def _rms_norm(x, g):
    return x * lax.rsqrt(jnp.mean(x * x, axis=-1, keepdims=True) + RMS_EPS) * g


def _split3(x):
    hi = x.astype(BF16).astype(F32)
    r = x - hi
    mid = r.astype(BF16).astype(F32)
    lo = (r - mid).astype(BF16).astype(F32)
    return hi, mid, lo


def _ffn_ln_kernel(x_ref, wa_ref, wb_ref, wd_ref, g_ref, b_ref, o_ref):
    rows = TILE // FF_ROW_GROUPS
    for r in range(FF_ROW_GROUPS):
        sl = slice(r * rows, (r + 1) * rows)
        x = x_ref[sl, :]
        xb = x.astype(BF16)
        a = _dot(xb, wa_ref[...])
        b = _dot(xb, wb_ref[...])
        gated = (a * jax.nn.sigmoid(a) * b).astype(BF16)
        y = ALPHA * x + 0.5 * _dot(gated, wd_ref[...])
        o_ref[sl, :] = _layer_norm(y, g_ref[...], b_ref[...])


def _resident(shape, index_map):
    return pl.BlockSpec(shape, index_map, pipeline_mode=pl.Buffered(1))


def _ffn_ln(x, w_in, w_down, g, b):
    n = x.shape[0]
    return pl.pallas_call(
        _ffn_ln_kernel,
        grid=(n // TILE,),
        in_specs=[
            pl.BlockSpec((TILE, D_MODEL), lambda i: (i, 0)),
            _resident((D_MODEL, D_FF), lambda i: (0, 0)),
            _resident((D_MODEL, D_FF), lambda i: (0, 1)),
            _resident((D_FF, D_MODEL), lambda i: (0, 0)),
            _resident((1, D_MODEL), lambda i: (0, 0)),
            _resident((1, D_MODEL), lambda i: (0, 0)),
        ],
        out_specs=pl.BlockSpec((TILE, D_MODEL), lambda i: (i, 0)),
        out_shape=jax.ShapeDtypeStruct((n, D_MODEL), F32),
        compiler_params=_cparams(("parallel",)),
        name="ffn_ln",
    )(x, w_in, w_in, w_down, g, b)


def _rope_kernel(pos_ref, inv_ref, cos_ref, sin_ref):
    ang = pos_ref[...].astype(F32) * inv_ref[...]
    c = jnp.cos(ang)
    s = jnp.sin(ang)
    cos_ref[...] = jnp.concatenate([c, c], axis=0)
    sin_ref[...] = jnp.concatenate([-s, s], axis=0)


def _rope_tables(positions):
    bsz, seq = positions.shape
    half = MLA_ROPE // 2
    inv_freq = (ROPE_THETA ** (-jnp.arange(half, dtype=F32) / half)).reshape(half, 1)
    ts = min(seq, 2048)
    spec = pl.BlockSpec((None, MLA_ROPE, ts), lambda b, s: (b, 0, s))
    return pl.pallas_call(
        _rope_kernel,
        grid=(bsz, seq // ts),
        in_specs=[pl.BlockSpec((None, 1, ts), lambda b, s: (b, 0, s)),
                  pl.BlockSpec((half, 1), lambda b, s: (0, 0))],
        out_specs=[spec, spec],
        out_shape=[jax.ShapeDtypeStruct((bsz, MLA_ROPE, seq), F32)] * 2,
        compiler_params=_cparams(("parallel", "parallel")),
        name="rope",
    )(positions.reshape(bsz, 1, seq), inv_freq)


def _attn_operand_specs(bsz, seq):
    nt = seq // TILE
    qt_spec = pl.BlockSpec((None, HEADS, None, QK_PAD, TILE), lambda b, s: (b, 0, s, 0, 0))
    k_spec = pl.BlockSpec((None, TILE, HEADS * QK_PAD), lambda b, s: (b, s, 0))
    vt_spec = pl.BlockSpec((None, HEADS, None, V_ROWS, TILE), lambda b, s: (b, 0, s, 0, 0))
    shapes = [jax.ShapeDtypeStruct((bsz, HEADS, nt, QK_PAD, TILE), BF16),
              jax.ShapeDtypeStruct((bsz, seq, HEADS * QK_PAD), BF16),
              jax.ShapeDtypeStruct((bsz, HEADS, nt, V_ROWS, TILE), BF16)]
    return [qt_spec, k_spec, vt_spec], shapes


def _store_vt(vt_ref, vt):
    row = lax.broadcasted_iota(jnp.int32, (V_ROWS - 64, TILE), 0)
    tail = jnp.where(row == 0, 1.0, 0.0).astype(BF16)
    for h in range(HEADS):
        vt_ref[h, 0:64, :] = vt[64 * h:64 * h + 64].astype(BF16)
        vt_ref[h, 64:V_ROWS, :] = tail


def _full(shape):
    return pl.BlockSpec(shape, lambda b, s: (0,) * len(shape))


def _mla_prep_kernel(h_ref, wc_ref, qg_ref, kvg_ref, wuq_ref, wuqsw_ref, wkpe_ref, wk_ref, wv_ref,
                     cos_ref, sin_ref, qt_ref, k_ref, vt_ref):
    scale = (MLA_NOPE + MLA_ROPE) ** -0.5 * LOG2E
    hb = h_ref[...].astype(BF16)
    c = _dot(hb, wc_ref[...])
    cqn = _rms_norm(c[:, :MLA_Q_LORA], qg_ref[...]).astype(BF16)
    kvn = _rms_norm(c[:, MLA_Q_LORA:], kvg_ref[...]).astype(BF16)
    cos_t = cos_ref[...]
    sin_t = sin_ref[...]

    q_t = _nt(wuq_ref[...], cqn)
    qsw_t = _nt(wuqsw_ref[...], cqn)
    dq = MLA_NOPE + MLA_ROPE
    for h in range(HEADS):
        qt_ref[h, 0:MLA_NOPE, :] = (q_t[dq * h:dq * h + MLA_NOPE] * scale).astype(BF16)
        rope = (q_t[dq * h + MLA_NOPE:dq * (h + 1)] * cos_t
                + qsw_t[MLA_ROPE * h:MLA_ROPE * (h + 1)] * sin_t)
        qt_ref[h, MLA_NOPE:dq, :] = (rope * scale).astype(BF16)
        qt_ref[h, dq:QK_PAD, :] = jnp.zeros((QK_PAD - dq, TILE), BF16)

    kpe_t = _nt(wkpe_ref[...], hb)
    kr_t = kpe_t[0:MLA_ROPE] * cos_t + kpe_t[MLA_ROPE:2 * MLA_ROPE] * sin_t
    blk_t = jnp.concatenate([jnp.zeros((MLA_NOPE, TILE), F32), kr_t,
                             jnp.zeros((QK_PAD - dq, TILE), F32)], axis=0)
    kr = blk_t.T
    k_nope = _dot(kvn, wk_ref[...])
    for h in range(HEADS):
        k_ref[:, QK_PAD * h:QK_PAD * (h + 1)] = (k_nope[:, QK_PAD * h:QK_PAD * (h + 1)] + kr).astype(BF16)

    _store_vt(vt_ref, _nt(wv_ref[...], kvn))


def _mla_prep(h, w, cos_t, sin_t):
    bsz, seq, _ = h.shape
    out_specs, out_shapes = _attn_operand_specs(bsz, seq)
    tab_spec = pl.BlockSpec((None, MLA_ROPE, TILE), lambda b, s: (b, 0, s))
    names = ["wc", "qg", "kvg", "wuq", "wuqsw", "wkpe", "wk", "wv"]
    return pl.pallas_call(
        _mla_prep_kernel,
        grid=(bsz, seq // TILE),
        in_specs=[pl.BlockSpec((None, TILE, D_MODEL), lambda b, s: (b, s, 0))]
        + [_full(w[n].shape) for n in names] + [tab_spec, tab_spec],
        out_specs=out_specs,
        out_shape=out_shapes,
        compiler_params=_cparams(("parallel", "parallel")),
        name="mla_prep",
    )(h, *[w[n] for n in names], cos_t, sin_t)


def _fox_prep_kernel(h_ref, wk_ref, wfl_ref, bfl_ref, wq_ref, wv_ref, tri_ref, pk_ref, pq_ref,
                     qt_ref, k_ref, vt_ref, carry_ref):
    @pl.when(pl.program_id(1) == 0)
    def _():
        carry_ref[...] = jnp.zeros_like(carry_ref)

    hb = h_ref[...].astype(BF16)
    lane = lax.broadcasted_iota(jnp.int32, (TILE, 128), 1)
    z = _dot(hb, wfl_ref[...]) + bfl_ref[...]
    log_f = jnp.minimum(z, 0.0) - jnp.log1p(jnp.exp(-jnp.abs(z)))
    hi, mid, lo = _split3(log_f)
    parts = jnp.where(lane < 8, hi, jnp.where(lane < 16, mid, jnp.where(lane < 24, lo, 0.0)))
    cum = _dot(tri_ref[...], parts.astype(BF16))
    f = cum + pltpu.roll(cum, 120, 1) + pltpu.roll(cum, 112, 1) + carry_ref[...]
    f = jnp.where(lane < 8, f, 0.0)
    carry_ref[...] = f[TILE - 1:TILE, :]

    fhi, fmid, flo = _split3(f * LOG2E)
    p = (fhi + pltpu.roll(fmid, 8, 1) + pltpu.roll(flo, 16, 1)
         + jnp.where(lane == 24, 1.0, 0.0)).astype(BF16)
    k_aug = _dot(hb, wk_ref[...]) + _dot(p, pk_ref[...])
    k_ref[...] = k_aug.astype(BF16)

    aug_t = _nt(pq_ref[...], p)
    q_t = _nt(wq_ref[...], hb)
    scale = FOX_DIM ** -0.5 * LOG2E
    for h in range(HEADS):
        qt_ref[h, 0:64, :] = (q_t[64 * h:64 * h + 64] * scale).astype(BF16)
        qt_ref[h, 64:80, :] = aug_t[16 * h:16 * h + 16].astype(BF16)
        qt_ref[h, 80:QK_PAD, :] = jnp.zeros((QK_PAD - 80, TILE), BF16)

    _store_vt(vt_ref, _nt(wv_ref[...], hb))


def _fox_prep(h, w):
    bsz, seq, _ = h.shape
    out_specs, out_shapes = _attn_operand_specs(bsz, seq)
    names = ["wk", "wfl", "bfl", "wq", "wv", "tri", "pk", "pq"]
    return pl.pallas_call(
        _fox_prep_kernel,
        grid=(bsz, seq // TILE),
        in_specs=[pl.BlockSpec((None, TILE, D_MODEL), lambda b, s: (b, s, 0))]
        + [_full(w[n].shape) for n in names],
        out_specs=out_specs,
        out_shape=out_shapes,
        scratch_shapes=[pltpu.VMEM((1, 128), F32)],
        compiler_params=_cparams(("parallel", "arbitrary")),
        name="fox_prep",
    )(h, *[w[n] for n in names])


MAX_LAG = 16.0
MAX_OVERSHOOT = 96.0
FAST_CHUNKS_PER_STEP = 4


def _attn_kernel(qt_ref, k_ref, vt_ref, o_ref, s0_ref, s1_ref, s2_ref, s3_ref, cm_ref, m_ref, acc_ref,
                 flag_ref, *, n_tiles):
    t = TILE
    nq = n_tiles // 2
    row = lax.broadcasted_iota(jnp.int32, (t, t), 0)
    col = lax.broadcasted_iota(jnp.int32, (t, t), 1)

    def logits(kblk, qtile, half, s_ref, slot):
        k = k_ref[pl.ds(pl.multiple_of(kblk * t, t), t), :]
        s = _dot(k, qt_ref[2 * qtile + half])
        s_ref[:, half * t:(half + 1) * t] = s
        cm_ref[slot, :, half * t:(half + 1) * t] = jnp.max(s, axis=0, keepdims=True)

    def softmax_pv(kblk, qtile, half, s_ref, slot, diagonal=False, first=False):
        cols = slice(half * t, (half + 1) * t)
        s = s_ref[:, cols]
        if diagonal:
            s = jnp.where(row > col, -jnp.inf, s)
            cm = jnp.max(s, axis=0, keepdims=True)
        else:
            cm = cm_ref[slot, :, cols]
        m_old = cm if first else m_ref[qtile, :, cols]
        m_new = cm if first else jnp.maximum(m_old, cm)
        p = jnp.exp2(s - m_new).astype(BF16)
        pv = _dot(vt_ref[kblk], p)
        if first:
            acc_ref[qtile, :, cols] = pv
        else:
            acc_ref[qtile, :, cols] = jnp.exp2(m_old - m_new) * acc_ref[qtile, :, cols] + pv
        m_ref[qtile, :, cols] = m_new

    def diag_logits(q, sa_ref, sb_ref, slot):
        logits(2 * q, q, 0, sa_ref, slot)
        logits(2 * q, q, 1, sa_ref, slot)
        logits(2 * q + 1, q, 1, sb_ref, slot + 1)

    def diag_softmax_pv(q, sa_ref, sb_ref, slot):
        softmax_pv(2 * q, q, 0, sa_ref, slot, diagonal=True, first=True)
        softmax_pv(2 * q, q, 1, sa_ref, slot, first=True)
        softmax_pv(2 * q + 1, q, 1, sb_ref, slot + 1, diagonal=True)

    def diagonal_phase():
        diag_logits(0, s0_ref, s1_ref, 0)

        def diag_pair(i, c):
            qa, qb = 2 * i, 2 * i + 1
            diag_logits(qb, s2_ref, s3_ref, 2)
            diag_softmax_pv(qa, s0_ref, s1_ref, 0)
            diag_logits(jnp.minimum(qa + 2, nq - 1), s0_ref, s1_ref, 0)
            diag_softmax_pv(qb, s2_ref, s3_ref, 2)
            return c

        lax.fori_loop(0, nq // 2, diag_pair, 0)

    def advance(q, j):
        wrap = j + 1 >= q
        return jnp.where(wrap, jnp.minimum(q + 1, nq - 1), q), jnp.where(wrap, 0, j + 1)

    n_chunks = nq * (nq - 1) // 2
    first_chunk = (jnp.int32(1), jnp.int32(0))

    def fast_unit(q, j, half):
        cols = slice(half * t, (half + 1) * t)
        k = k_ref[pl.ds(pl.multiple_of(j * 2 * t, 2 * t), 2 * t), :]
        s = _dot(k, qt_ref[2 * q + half])
        cm = jnp.max(s, axis=0, keepdims=True)
        m_old = m_ref[q, :, cols]
        p = jnp.exp2(s - m_old).astype(BF16)
        vt = jnp.concatenate([vt_ref[2 * j], vt_ref[2 * j + 1]], axis=1)
        pv = _dot(vt, p)
        m_new = jnp.maximum(m_old, cm - MAX_LAG)
        acc_ref[q, :, cols] = (acc_ref[q, :, cols] + pv) * jnp.exp2(m_old - m_new)
        m_ref[q, :, cols] = m_new
        flag_ref[:, cols] = jnp.maximum(flag_ref[:, cols], cm - m_old)

    def fast_phase():
        def step(i, c):
            for _ in range(FAST_CHUNKS_PER_STEP):
                for half in range(2):
                    fast_unit(c[0], c[1], half)
                c = advance(*c)
            return c

        c = lax.fori_loop(0, n_chunks // FAST_CHUNKS_PER_STEP, step, first_chunk)
        for _ in range(n_chunks % FAST_CHUNKS_PER_STEP):
            for half in range(2):
                fast_unit(c[0], c[1], half)
            c = advance(*c)

    def safe_chunk(q, j, qn, jn):
        for half in range(2):
            logits(2 * j + 1, q, half, s1_ref, 1)
        for half in range(2):
            softmax_pv(2 * j, q, half, s0_ref, 0)
        for half in range(2):
            logits(2 * jn, qn, half, s0_ref, 0)
        for half in range(2):
            softmax_pv(2 * j + 1, q, half, s1_ref, 1)

    def safe_phase():
        for half in range(2):
            logits(0, 1, half, s0_ref, 0)

        def step(i, c):
            nxt = advance(*c)
            safe_chunk(*c, *nxt)
            return nxt

        lax.fori_loop(0, n_chunks, step, first_chunk)

    def finalize():
        def body(q, c):
            for half in range(2):
                cols = slice(half * t, (half + 1) * t)
                o_ref[2 * q + half] = (acc_ref[q, 0:64, cols] / acc_ref[q, 64:65, cols]).astype(BF16)
            return c

        lax.fori_loop(0, nq, body, 0)

    flag_ref[...] = jnp.full(flag_ref.shape, -jnp.inf, F32)
    diagonal_phase()
    if n_chunks:
        fast_phase()
    finalize()

    @pl.when(jnp.max(flag_ref[...]) > MAX_OVERSHOOT)
    def _():
        diagonal_phase()
        if n_chunks:
            safe_phase()
        finalize()


def _attention(qt, k, vt):
    bsz, heads, nt = qt.shape[:3]
    seq = k.shape[1]
    return pl.pallas_call(
        functools.partial(_attn_kernel, n_tiles=nt),
        grid=(bsz, heads),
        in_specs=[pl.BlockSpec((None, None, nt, QK_PAD, TILE), lambda b, h: (b, h, 0, 0, 0)),
                  pl.BlockSpec((None, seq, QK_PAD), lambda b, h: (b, 0, h)),
                  pl.BlockSpec((None, None, nt, V_ROWS, TILE), lambda b, h: (b, h, 0, 0, 0))],
        out_specs=pl.BlockSpec((None, None, nt, 64, TILE), lambda b, h: (b, h, 0, 0, 0)),
        out_shape=jax.ShapeDtypeStruct((bsz, heads, nt, 64, TILE), BF16),
        scratch_shapes=[pltpu.VMEM((TILE, 2 * TILE), F32)] * 4
        + [pltpu.VMEM((4, 1, 2 * TILE), F32), pltpu.VMEM((nt // 2, 1, 2 * TILE), F32),
           pltpu.VMEM((nt // 2, V_ROWS, 2 * TILE), F32), pltpu.VMEM((1, 2 * TILE), F32)],
        compiler_params=_cparams(("parallel", "parallel")),
        name="attn",
    )(qt, k, vt)


def _mem_kv_kernel(mem_ref, w_ref, mk_ref, mv_ref):
    kv = _dot(mem_ref[...].astype(BF16), w_ref[...])
    mk_ref[...] = kv[:, :MEM_W].astype(BF16)
    mv_ref[...] = kv[:, MEM_W:].astype(BF16)


def _mem_kv(mem, w):
    bsz = mem.shape[0]
    spec = pl.BlockSpec((None, MEM_LEN, MEM_W), lambda b: (b, 0, 0))
    return pl.pallas_call(
        _mem_kv_kernel,
        grid=(bsz,),
        in_specs=[pl.BlockSpec((None, MEM_LEN, D_MODEL), lambda b: (b, 0, 0)),
                  pl.BlockSpec(w.shape, lambda b: (0, 0))],
        out_specs=[spec, spec],
        out_shape=[jax.ShapeDtypeStruct((bsz, MEM_LEN, MEM_W), BF16)] * 2,
        compiler_params=_cparams(("parallel",)),
        name="mem_kv",
    )(mem, w)


def _mem_attn_kernel(h_ref, wq_ref, mk_ref, mv_ref, o_ref):
    scale = MEM_DIM ** -0.5
    mq = _dot(h_ref[...].astype(BF16), wq_ref[...]).astype(BF16)
    for h in range(MEM_HEADS):
        sl = slice(MEM_DIM * h, MEM_DIM * (h + 1))
        s = _nt(mq[:, sl], mk_ref[:, sl]) * scale
        e = jnp.exp(s - jnp.max(s, axis=-1, keepdims=True))
        o = _dot(e.astype(BF16), mv_ref[:, sl]) / jnp.sum(e, axis=-1, keepdims=True)
        o_ref[:, sl] = o.astype(BF16)


def _mem_attn(h, wq, mk, mv):
    bsz, seq, _ = h.shape
    kv_spec = pl.BlockSpec((None, MEM_LEN, MEM_W), lambda b, s: (b, 0, 0))
    return pl.pallas_call(
        _mem_attn_kernel,
        grid=(bsz, seq // TILE),
        in_specs=[pl.BlockSpec((None, TILE, D_MODEL), lambda b, s: (b, s, 0)),
                  _full(wq.shape), kv_spec, kv_spec],
        out_specs=pl.BlockSpec((None, TILE, MEM_W), lambda b, s: (b, s, 0)),
        out_shape=jax.ShapeDtypeStruct((bsz, seq, MEM_W), BF16),
        compiler_params=_cparams(("parallel", "parallel")),
        name="mem_attn",
    )(h, wq, mk, mv)


def _merge_kernel(h_ref, omla_ref, ofox_ref, omem_ref, wg_ref, bg_ref, wbm_ref, wbf_ref, wbc_ref,
                  wo_ref, g_ref, b_ref, o_ref):
    o_mla = omla_ref[...].reshape(MLA_W, TILE)
    o_fox = ofox_ref[...].reshape(FOX_W, TILE)
    rows = TILE // MERGE_ROW_GROUPS
    for r in range(MERGE_ROW_GROUPS):
        sl = slice(r * rows, (r + 1) * rows)
        h = h_ref[sl, :]
        gates = jax.nn.sigmoid(_dot(h.astype(BF16), wg_ref[...]) + bg_ref[...])
        a_mla = _tn(o_mla[:, sl], wbm_ref[...])
        a_fox = _tn(o_fox[:, sl], wbf_ref[...])
        a_mem = _dot(omem_ref[sl, :], wbc_ref[...])
        merged = (gates[:, :D_MODEL] * a_mla + gates[:, D_MODEL:2 * D_MODEL] * a_fox
                  + gates[:, 2 * D_MODEL:] * a_mem)
        mix = _dot(merged.astype(BF16), wo_ref[...])
        o_ref[sl, :] = _layer_norm(ALPHA * h + mix, g_ref[...], b_ref[...])


def _merge(h, o_mla, o_fox, o_mem, w):
    bsz, seq, _ = h.shape
    ot_spec = pl.BlockSpec((None, HEADS, None, 64, TILE), lambda b, s: (b, 0, s, 0, 0))
    names = ["wg", "bg", "wbm", "wbf", "wbc", "wo", "g", "b"]
    return pl.pallas_call(
        _merge_kernel,
        grid=(bsz, seq // TILE),
        in_specs=[pl.BlockSpec((None, TILE, D_MODEL), lambda b, s: (b, s, 0)), ot_spec, ot_spec,
                  pl.BlockSpec((None, TILE, MEM_W), lambda b, s: (b, s, 0))]
        + [_full(w[n].shape) for n in names],
        out_specs=pl.BlockSpec((None, TILE, D_MODEL), lambda b, s: (b, s, 0)),
        out_shape=jax.ShapeDtypeStruct((bsz, seq, D_MODEL), F32),
        compiler_params=_cparams(("parallel", "parallel")),
        name="merge",
    )(h, o_mla, o_fox, o_mem, *[w[n] for n in names])


def _placement_matrices():
    pk = np.zeros((128, HEADS * QK_PAD), np.float32)
    pq = np.zeros((HEADS * 16, 128), np.float32)
    for h in range(HEADS):
        for part in range(3):
            pk[8 * part + h, QK_PAD * h + 64 + part] = -1.0
            pk[24, QK_PAD * h + 67 + part] = 1.0
            pq[16 * h + part, 24] = 1.0
            pq[16 * h + 3 + part, 8 * part + h] = 1.0
    tri = np.tril(np.ones((TILE, TILE), np.float32))
    return jnp.asarray(pk, BF16), jnp.asarray(pq, BF16), jnp.asarray(tri, BF16)


def _pad_heads(w, width):
    k = w.shape[0]
    w = w.reshape(k, HEADS, width)
    return jnp.pad(w, ((0, 0), (0, 0), (0, QK_PAD - width))).reshape(k, HEADS * QK_PAD)


def _swap_halves(w, axis):
    a, b = jnp.split(w, 2, axis=axis)
    return jnp.concatenate([b, a], axis=axis)


def kernel(x, mem, positions, ln1_g, ln1_b, ffn1_w_in, ffn1_w_down, w_in, b_gate, mla_q_norm, mla_w_uq, mla_kv_norm, mla_w_ukv, fox_b_f, mem_w_kv, w_br_mla, w_br_fox, w_br_mem, w_out, ln2_g, ln2_b, ffn2_w_in, ffn2_w_down, ln3_g, ln3_b):
    bsz, seq, _ = x.shape
    n = bsz * seq
    row = lambda v: v.reshape(1, -1).astype(F32)
    bf = lambda v: v.astype(BF16)

    sizes = (MLA_Q_LORA, MLA_KV_LORA, MLA_ROPE, FOX_W, FOX_W, FOX_W, FOX_HEADS, MEM_W)
    offs = np.cumsum((0,) + sizes).tolist()
    wi = w_in[0]
    w_cq, w_ckv, w_kpe, w_fq, w_fk, w_fv, w_fl, w_mq = (wi[:, offs[i]:offs[i + 1]] for i in range(8))
    w_gates = wi[:, offs[8]:]

    uq = mla_w_uq[0].reshape(MLA_Q_LORA, MLA_HEADS, MLA_NOPE + MLA_ROPE)
    uq_rope_sw = _swap_halves(uq[:, :, MLA_NOPE:], axis=2).reshape(MLA_Q_LORA, MLA_HEADS * MLA_ROPE)
    ukv = mla_w_ukv[0].reshape(MLA_KV_LORA, MLA_HEADS, MLA_NOPE + MLA_V)
    w_kpe_t = jnp.concatenate([w_kpe.T, _swap_halves(w_kpe, axis=1).T,
                               jnp.zeros((QK_PAD - 2 * MLA_ROPE, D_MODEL), F32)], axis=0)
    mla_w = {
        "wc": bf(jnp.concatenate([w_cq, w_ckv], axis=1)),
        "qg": row(mla_q_norm[0]), "kvg": row(mla_kv_norm[0]),
        "wuq": bf(mla_w_uq[0].T), "wuqsw": bf(uq_rope_sw.T), "wkpe": bf(w_kpe_t),
        "wk": bf(_pad_heads(ukv[:, :, :MLA_NOPE].reshape(MLA_KV_LORA, -1), MLA_NOPE)),
        "wv": bf(ukv[:, :, MLA_NOPE:].reshape(MLA_KV_LORA, -1).T),
    }

    pk, pq, tri = _placement_matrices()
    w_fl3 = jnp.concatenate([w_fl, w_fl, w_fl, jnp.zeros((D_MODEL, 128 - 3 * FOX_HEADS), F32)], axis=1)
    b_fl3 = jnp.concatenate([fox_b_f[0]] * 3 + [jnp.zeros((128 - 3 * FOX_HEADS,), F32)]).reshape(1, 128)
    fox_w = {
        "wk": bf(_pad_heads(w_fk, FOX_DIM)), "wfl": bf(w_fl3), "bfl": b_fl3,
        "wq": bf(w_fq.T), "wv": bf(w_fv.T), "tri": tri, "pk": pk, "pq": pq,
    }

    merge_w = {
        "wg": bf(w_gates), "bg": row(b_gate[0]), "wbm": bf(w_br_mla[0]), "wbf": bf(w_br_fox[0]),
        "wbc": bf(w_br_mem[0]), "wo": bf(w_out[0]), "g": row(ln2_g[0]), "b": row(ln2_b[0]),
    }

    h1 = _ffn_ln(x.reshape(n, D_MODEL), bf(ffn1_w_in[0]), bf(ffn1_w_down[0]), row(ln1_g[0]), row(ln1_b[0]))
    h1 = h1.reshape(bsz, seq, D_MODEL)

    cos_t, sin_t = _rope_tables(positions)
    o_mla = _attention(*_mla_prep(h1, mla_w, cos_t, sin_t))
    o_fox = _attention(*_fox_prep(h1, fox_w))
    mk, mv = _mem_kv(mem, bf(mem_w_kv[0]))
    o_mem = _mem_attn(h1, bf(w_mq), mk, mv)

    h2 = _merge(h1, o_mla, o_fox, o_mem, merge_w)
    out = _ffn_ln(h2.reshape(n, D_MODEL), bf(ffn2_w_in[0]), bf(ffn2_w_down[0]), row(ln3_g[0]), row(ln3_b[0]))
    return out.reshape(bsz, seq, D_MODEL)
```

```python
import functools

import numpy as np
import jax
import jax.numpy as jnp
from jax import lax
from jax.experimental import pallas as pl
from jax.experimental.pallas import tpu as pltpu

F32 = jnp.float32
BF16 = jnp.bfloat16

D_MODEL = 1024
MEM_LEN = 256
MLA_HEADS = 8
MLA_NOPE = 64
MLA_ROPE = 32
MLA_V = 64
MLA_Q_LORA = 384
MLA_KV_LORA = 256
FOX_HEADS = 8
FOX_DIM = 64
MEM_HEADS = 4
MEM_DIM = 128
D_FF = 2816
ROPE_THETA = 10000.0
LN_EPS = 1e-5
RMS_EPS = 1e-6
DEPTH = 1
ALPHA = (2.0 * DEPTH) ** 0.25
MLA_W = MLA_HEADS * MLA_V
FOX_W = FOX_HEADS * FOX_DIM
MEM_W = MEM_HEADS * MEM_DIM

HEADS = 8
QK_PAD = 128
V_ROWS = 80
TILE = 512
FF_ROW_GROUPS = 2
MERGE_ROW_GROUPS = 2
VMEM_LIMIT = 56 * 1024 * 1024
LOG2E = 1.4426950408889634


def _cparams(sem):
    return pltpu.CompilerParams(dimension_semantics=sem, vmem_limit_bytes=VMEM_LIMIT)


def _nt(a, b):
    return lax.dot_general(a, b, (((1,), (1,)), ((), ())), preferred_element_type=F32)


def _tn(a, b):
    return lax.dot_general(a, b, (((0,), (0,)), ((), ())), preferred_element_type=F32)


def _dot(a, b):
    return jnp.dot(a, b, preferred_element_type=F32)


def _layer_norm(y, g, b):
    mu = jnp.mean(y, axis=-1, keepdims=True)
    d = y - mu
    var = jnp.mean(d * d, axis=-1, keepdims=True)
    return d * lax.rsqrt(var + LN_EPS) * g + b


def _rms_norm(x, g):
    return x * lax.rsqrt(jnp.mean(x * x, axis=-1, keepdims=True) + RMS_EPS) * g


def _split3(x):
    hi = x.astype(BF16).astype(F32)
    r = x - hi
    mid = r.astype(BF16).astype(F32)
    lo = (r - mid).astype(BF16).astype(F32)
    return hi, mid, lo


def _ffn_ln_kernel(x_ref, wa_ref, wb_ref, wd_ref, g_ref, b_ref, o_ref):
    rows = TILE // FF_ROW_GROUPS
    for r in range(FF_ROW_GROUPS):
        sl = slice(r * rows, (r + 1) * rows)
        x = x_ref[sl, :]
        xb = x.astype(BF16)
        a = _dot(xb, wa_ref[...])
        b = _dot(xb, wb_ref[...])
        gated = (a * jax.nn.sigmoid(a) * b).astype(BF16)
        y = ALPHA * x + 0.5 * _dot(gated, wd_ref[...])
        o_ref[sl, :] = _layer_norm(y, g_ref[...], b_ref[...])


def _resident(shape, index_map):
    return pl.BlockSpec(shape, index_map, pipeline_mode=pl.Buffered(1))


def _ffn_ln(x, w_in, w_down, g, b):
    n = x.shape[0]
    return pl.pallas_call(
        _ffn_ln_kernel,
        grid=(n // TILE,),
        in_specs=[
            pl.BlockSpec((TILE, D_MODEL), lambda i: (i, 0)),
            _resident((D_MODEL, D_FF), lambda i: (0, 0)),
            _resident((D_MODEL, D_FF), lambda i: (0, 1)),
            _resident((D_FF, D_MODEL), lambda i: (0, 0)),
            _resident((1, D_MODEL), lambda i: (0, 0)),
            _resident((1, D_MODEL), lambda i: (0, 0)),
        ],
        out_specs=pl.BlockSpec((TILE, D_MODEL), lambda i: (i, 0)),
        out_shape=jax.ShapeDtypeStruct((n, D_MODEL), F32),
        compiler_params=_cparams(("parallel",)),
        name="ffn_ln",
    )(x, w_in, w_in, w_down, g, b)


def _rope_kernel(pos_ref, inv_ref, cos_ref, sin_ref):
    ang = pos_ref[...].astype(F32) * inv_ref[...]
    c = jnp.cos(ang)
    s = jnp.sin(ang)
    cos_ref[...] = jnp.concatenate([c, c], axis=0)
    sin_ref[...] = jnp.concatenate([-s, s], axis=0)


def _rope_tables(positions):
    bsz, seq = positions.shape
    half = MLA_ROPE // 2
    inv_freq = (ROPE_THETA ** (-jnp.arange(half, dtype=F32) / half)).reshape(half, 1)
    ts = min(seq, 2048)
    spec = pl.BlockSpec((None, MLA_ROPE, ts), lambda b, s: (b, 0, s))
    return pl.pallas_call(
        _rope_kernel,
        grid=(bsz, seq // ts),
        in_specs=[pl.BlockSpec((None, 1, ts), lambda b, s: (b, 0, s)),
                  pl.BlockSpec((half, 1), lambda b, s: (0, 0))],
        out_specs=[spec, spec],
        out_shape=[jax.ShapeDtypeStruct((bsz, MLA_ROPE, seq), F32)] * 2,
        compiler_params=_cparams(("parallel", "parallel")),
        name="rope",
    )(positions.reshape(bsz, 1, seq), inv_freq)


def _attn_operand_specs(bsz, seq):
    nt = seq // TILE
    qt_spec = pl.BlockSpec((None, HEADS, None, QK_PAD, TILE), lambda b, s: (b, 0, s, 0, 0))
    k_spec = pl.BlockSpec((None, TILE, HEADS * QK_PAD), lambda b, s: (b, s, 0))
    vt_spec = pl.BlockSpec((None, HEADS, None, V_ROWS, TILE), lambda b, s: (b, 0, s, 0, 0))
    shapes = [jax.ShapeDtypeStruct((bsz, HEADS, nt, QK_PAD, TILE), BF16),
              jax.ShapeDtypeStruct((bsz, seq, HEADS * QK_PAD), BF16),
              jax.ShapeDtypeStruct((bsz, HEADS, nt, V_ROWS, TILE), BF16)]
    return [qt_spec, k_spec, vt_spec], shapes


def _store_vt(vt_ref, vt):
    row = lax.broadcasted_iota(jnp.int32, (V_ROWS - 64, TILE), 0)
    tail = jnp.where(row == 0, 1.0, 0.0).astype(BF16)
    for h in range(HEADS):
        vt_ref[h, 0:64, :] = vt[64 * h:64 * h + 64].astype(BF16)
        vt_ref[h, 64:V_ROWS, :] = tail


def _full(shape):
    return pl.BlockSpec(shape, lambda b, s: (0,) * len(shape))


def _mla_prep_kernel(h_ref, wc_ref, qg_ref, kvg_ref, wuq_ref, wuqsw_ref, wkpe_ref, wk_ref, wv_ref,
                     cos_ref, sin_ref, qt_ref, k_ref, vt_ref):
    scale = (MLA_NOPE + MLA_ROPE) ** -0.5 * LOG2E
    hb = h_ref[...].astype(BF16)
    c = _dot(hb, wc_ref[...])
    cqn = _rms_norm(c[:, :MLA_Q_LORA], qg_ref[...]).astype(BF16)
    kvn = _rms_norm(c[:, MLA_Q_LORA:], kvg_ref[...]).astype(BF16)
    cos_t = cos_ref[...]
    sin_t = sin_ref[...]

    q_t = _nt(wuq_ref[...], cqn)
    qsw_t = _nt(wuqsw_ref[...], cqn)
    dq = MLA_NOPE + MLA_ROPE
    for h in range(HEADS):
        qt_ref[h, 0:MLA_NOPE, :] = (q_t[dq * h:dq * h + MLA_NOPE] * scale).astype(BF16)
        rope = (q_t[dq * h + MLA_NOPE:dq * (h + 1)] * cos_t
                + qsw_t[MLA_ROPE * h:MLA_ROPE * (h + 1)] * sin_t)
        qt_ref[h, MLA_NOPE:dq, :] = (rope * scale).astype(BF16)
        qt_ref[h, dq:QK_PAD, :] = jnp.zeros((QK_PAD - dq, TILE), BF16)

    kpe_t = _nt(wkpe_ref[...], hb)
    kr_t = kpe_t[0:MLA_ROPE] * cos_t + kpe_t[MLA_ROPE:2 * MLA_ROPE] * sin_t
    blk_t = jnp.concatenate([jnp.zeros((MLA_NOPE, TILE), F32), kr_t,
                             jnp.zeros((QK_PAD - dq, TILE), F32)], axis=0)
    kr = blk_t.T
    k_nope = _dot(kvn, wk_ref[...])
    for h in range(HEADS):
        k_ref[:, QK_PAD * h:QK_PAD * (h + 1)] = (k_nope[:, QK_PAD * h:QK_PAD * (h + 1)] + kr).astype(BF16)

    _store_vt(vt_ref, _nt(wv_ref[...], kvn))


MLA_WEIGHT_NAMES = ("wc", "qg", "kvg", "wuq", "wuqsw", "wkpe", "wk", "wv")
FOX_WEIGHT_NAMES = ("wk", "wfl", "bfl", "wq", "wv", "tri", "pk", "pq")


def _fox_prep_kernel(h_ref, wk_ref, wfl_ref, bfl_ref, wq_ref, wv_ref, tri_ref, pk_ref, pq_ref,
                     qt_ref, k_ref, vt_ref, carry_ref):
    @pl.when(pl.program_id(1) == 0)
    def _():
        carry_ref[...] = jnp.zeros_like(carry_ref)

    hb = h_ref[...].astype(BF16)
    lane = lax.broadcasted_iota(jnp.int32, (TILE, 128), 1)
    z = _dot(hb, wfl_ref[...]) + bfl_ref[...]
    log_f = jnp.minimum(z, 0.0) - jnp.log1p(jnp.exp(-jnp.abs(z)))
    hi, mid, lo = _split3(log_f)
    parts = jnp.where(lane < 8, hi, jnp.where(lane < 16, mid, jnp.where(lane < 24, lo, 0.0)))
    cum = _dot(tri_ref[...], parts.astype(BF16))
    f = cum + pltpu.roll(cum, 120, 1) + pltpu.roll(cum, 112, 1) + carry_ref[...]
    f = jnp.where(lane < 8, f, 0.0)
    carry_ref[...] = f[TILE - 1:TILE, :]

    fhi, fmid, flo = _split3(f * LOG2E)
    p = (fhi + pltpu.roll(fmid, 8, 1) + pltpu.roll(flo, 16, 1)
         + jnp.where(lane == 24, 1.0, 0.0)).astype(BF16)
    k_aug = _dot(hb, wk_ref[...]) + _dot(p, pk_ref[...])
    k_ref[...] = k_aug.astype(BF16)

    aug_t = _nt(pq_ref[...], p)
    q_t = _nt(wq_ref[...], hb)
    scale = FOX_DIM ** -0.5 * LOG2E
    for h in range(HEADS):
        qt_ref[h, 0:64, :] = (q_t[64 * h:64 * h + 64] * scale).astype(BF16)
        qt_ref[h, 64:80, :] = aug_t[16 * h:16 * h + 16].astype(BF16)
        qt_ref[h, 80:QK_PAD, :] = jnp.zeros((QK_PAD - 80, TILE), BF16)

    _store_vt(vt_ref, _nt(wv_ref[...], hb))


MAX_LAG = 16.0
MAX_OVERSHOOT = 96.0
FAST_CHUNKS_PER_STEP = 4


def _attn_kernel(qt_ref, k_ref, vt_ref, o_ref, s0_ref, s1_ref, s2_ref, s3_ref, cm_ref, m_ref, acc_ref,
                 flag_ref, *, n_tiles):
    t = TILE
    nq = n_tiles // 2
    row = lax.broadcasted_iota(jnp.int32, (t, t), 0)
    col = lax.broadcasted_iota(jnp.int32, (t, t), 1)

    def logits(kblk, qtile, half, s_ref, slot):
        k = k_ref[pl.ds(pl.multiple_of(kblk * t, t), t), :]
        s = _dot(k.astype(F32), qt_ref[2 * qtile + half].astype(F32))
        s_ref[:, half * t:(half + 1) * t] = s
        cm_ref[slot, :, half * t:(half + 1) * t] = jnp.max(s, axis=0, keepdims=True)

    def softmax_pv(kblk, qtile, half, s_ref, slot, diagonal=False, first=False):
        cols = slice(half * t, (half + 1) * t)
        s = s_ref[:, cols]
        if diagonal:
            s = jnp.where(row > col, -jnp.inf, s)
            cm = jnp.max(s, axis=0, keepdims=True)
        else:
            cm = cm_ref[slot, :, cols]
        m_old = cm if first else m_ref[qtile, :, cols]
        m_new = cm if first else jnp.maximum(m_old, cm)
        p = jnp.exp2(s - m_new).astype(BF16)
        pv = _dot(vt_ref[kblk], p)
        if first:
            acc_ref[qtile, :, cols] = pv
        else:
            acc_ref[qtile, :, cols] = jnp.exp2(m_old - m_new) * acc_ref[qtile, :, cols] + pv
        m_ref[qtile, :, cols] = m_new

    def diag_logits(q, sa_ref, sb_ref, slot):
        logits(2 * q, q, 0, sa_ref, slot)
        logits(2 * q, q, 1, sa_ref, slot)
        logits(2 * q + 1, q, 1, sb_ref, slot + 1)

    def diag_softmax_pv(q, sa_ref, sb_ref, slot):
        softmax_pv(2 * q, q, 0, sa_ref, slot, diagonal=True, first=True)
        softmax_pv(2 * q, q, 1, sa_ref, slot, first=True)
        softmax_pv(2 * q + 1, q, 1, sb_ref, slot + 1, diagonal=True)

    def diagonal_phase():
        diag_logits(0, s0_ref, s1_ref, 0)

        def diag_pair(i, c):
            qa, qb = 2 * i, 2 * i + 1
            diag_logits(qb, s2_ref, s3_ref, 2)
            diag_softmax_pv(qa, s0_ref, s1_ref, 0)
            diag_logits(jnp.minimum(qa + 2, nq - 1), s0_ref, s1_ref, 0)
            diag_softmax_pv(qb, s2_ref, s3_ref, 2)
            return c

        lax.fori_loop(0, nq // 2, diag_pair, 0)

    def advance(q, j):
        wrap = j + 1 >= q
        return jnp.where(wrap, jnp.minimum(q + 1, nq - 1), q), jnp.where(wrap, 0, j + 1)

    n_chunks = nq * (nq - 1) // 2
    first_chunk = (jnp.int32(1), jnp.int32(0))

    def fast_unit(q, j, half):
        cols = slice(half * t, (half + 1) * t)
        k = k_ref[pl.ds(pl.multiple_of(j * 2 * t, 2 * t), 2 * t), :]
        s = _dot(k.astype(F32), qt_ref[2 * q + half].astype(F32))
        cm = jnp.max(s, axis=0, keepdims=True)
        m_old = m_ref[q, :, cols]
        p = jnp.exp2(s - m_old).astype(BF16)
        vt = jnp.concatenate([vt_ref[2 * j], vt_ref[2 * j + 1]], axis=1)
        pv = _dot(vt, p)
        m_new = jnp.maximum(m_old, cm - MAX_LAG)
        acc_ref[q, :, cols] = (acc_ref[q, :, cols] + pv) * jnp.exp2(m_old - m_new)
        m_ref[q, :, cols] = m_new
        flag_ref[:, cols] = jnp.maximum(flag_ref[:, cols], cm - m_old)

    def fast_phase():
        def step(i, c):
            for _ in range(FAST_CHUNKS_PER_STEP):
                for half in range(2):
                    fast_unit(c[0], c[1], half)
                c = advance(*c)
            return c

        c = lax.fori_loop(0, n_chunks // FAST_CHUNKS_PER_STEP, step, first_chunk)
        for _ in range(n_chunks % FAST_CHUNKS_PER_STEP):
            for half in range(2):
                fast_unit(c[0], c[1], half)
            c = advance(*c)

    def safe_chunk(q, j, qn, jn):
        for half in range(2):
            logits(2 * j + 1, q, half, s1_ref, 1)
        for half in range(2):
            softmax_pv(2 * j, q, half, s0_ref, 0)
        for half in range(2):
            logits(2 * jn, qn, half, s0_ref, 0)
        for half in range(2):
            softmax_pv(2 * j + 1, q, half, s1_ref, 1)

    def safe_phase():
        for half in range(2):
            logits(0, 1, half, s0_ref, 0)

        def step(i, c):
            nxt = advance(*c)
            safe_chunk(*c, *nxt)
            return nxt

        lax.fori_loop(0, n_chunks, step, first_chunk)

    def finalize():
        def body(q, c):
            for half in range(2):
                cols = slice(half * t, (half + 1) * t)
                o_ref[2 * q + half] = (acc_ref[q, 0:64, cols] / acc_ref[q, 64:65, cols]).astype(BF16)
            return c

        lax.fori_loop(0, nq, body, 0)

    flag_ref[...] = jnp.full(flag_ref.shape, -jnp.inf, F32)
    diagonal_phase()
    if n_chunks:
        fast_phase()
    finalize()

    @pl.when(jnp.max(flag_ref[...]) > MAX_OVERSHOOT)
    def _():
        diagonal_phase()
        if n_chunks:
            safe_phase()
        finalize()


def _attention(qt, k, vt):
    bsz, heads, nt = qt.shape[:3]
    seq = k.shape[1]
    return pl.pallas_call(
        functools.partial(_attn_kernel, n_tiles=nt),
        grid=(bsz, heads),
        in_specs=[pl.BlockSpec((None, None, nt, QK_PAD, TILE), lambda b, h: (b, h, 0, 0, 0)),
                  pl.BlockSpec((None, seq, QK_PAD), lambda b, h: (b, 0, h)),
                  pl.BlockSpec((None, None, nt, V_ROWS, TILE), lambda b, h: (b, h, 0, 0, 0))],
        out_specs=pl.BlockSpec((None, None, nt, 64, TILE), lambda b, h: (b, h, 0, 0, 0)),
        out_shape=jax.ShapeDtypeStruct((bsz, heads, nt, 64, TILE), BF16),
        scratch_shapes=[pltpu.VMEM((TILE, 2 * TILE), F32)] * 4
        + [pltpu.VMEM((4, 1, 2 * TILE), F32), pltpu.VMEM((nt // 2, 1, 2 * TILE), F32),
           pltpu.VMEM((nt // 2, V_ROWS, 2 * TILE), F32), pltpu.VMEM((1, 2 * TILE), F32)],
        compiler_params=_cparams(("parallel", "parallel")),
        name="attn",
    )(qt, k, vt)


def _mem_kv_kernel(mem_ref, w_ref, mk_ref, mv_ref):
    kv = _dot(mem_ref[...].astype(BF16), w_ref[...])
    mk_ref[...] = kv[:, :MEM_W].astype(BF16)
    mv_ref[...] = kv[:, MEM_W:].astype(BF16)


def _mem_kv(mem, w):
    bsz = mem.shape[0]
    spec = pl.BlockSpec((None, MEM_LEN, MEM_W), lambda b: (b, 0, 0))
    return pl.pallas_call(
        _mem_kv_kernel,
        grid=(bsz,),
        in_specs=[pl.BlockSpec((None, MEM_LEN, D_MODEL), lambda b: (b, 0, 0)),
                  pl.BlockSpec(w.shape, lambda b: (0, 0))],
        out_specs=[spec, spec],
        out_shape=[jax.ShapeDtypeStruct((bsz, MEM_LEN, MEM_W), BF16)] * 2,
        compiler_params=_cparams(("parallel",)),
        name="mem_kv",
    )(mem, w)


def _mem_attn_kernel(h_ref, wq_ref, mk_ref, mv_ref, o_ref):
    scale = MEM_DIM ** -0.5
    mq = _dot(h_ref[...].astype(BF16), wq_ref[...]).astype(BF16)
    for h in range(MEM_HEADS):
        sl = slice(MEM_DIM * h, MEM_DIM * (h + 1))
        s = _nt(mq[:, sl], mk_ref[:, sl]) * scale
        e = jnp.exp(s - jnp.max(s, axis=-1, keepdims=True))
        o = _dot(e.astype(BF16), mv_ref[:, sl]) / jnp.sum(e, axis=-1, keepdims=True)
        o_ref[:, sl] = o.astype(BF16)


def _mixer_prep_kernel(*refs):
    n_mla, n_fox = len(MLA_WEIGHT_NAMES), len(FOX_WEIGHT_NAMES)
    refs = list(refs)
    h_ref = refs.pop(0)
    mla_in = [refs.pop(0) for _ in range(n_mla + 2)]
    fox_in = [refs.pop(0) for _ in range(n_fox)]
    mem_in = [refs.pop(0) for _ in range(3)]
    mla_out = [refs.pop(0) for _ in range(3)]
    fox_out = [refs.pop(0) for _ in range(3)]
    omem_ref, carry_ref = refs
    _mla_prep_kernel(h_ref, *mla_in, *mla_out)
    _fox_prep_kernel(h_ref, *fox_in, *fox_out, carry_ref)
    _mem_attn_kernel(h_ref, *mem_in, omem_ref)


def _mixer_prep(h, mla_w, cos_t, sin_t, fox_w, wq, mk, mv):
    bsz, seq, _ = h.shape
    op_specs, op_shapes = _attn_operand_specs(bsz, seq)
    tab_spec = pl.BlockSpec((None, MLA_ROPE, TILE), lambda b, s: (b, 0, s))
    kv_spec = pl.BlockSpec((None, MEM_LEN, MEM_W), lambda b, s: (b, 0, 0))
    mla_args = [mla_w[n] for n in MLA_WEIGHT_NAMES]
    fox_args = [fox_w[n] for n in FOX_WEIGHT_NAMES]
    outs = pl.pallas_call(
        _mixer_prep_kernel,
        grid=(bsz, seq // TILE),
        in_specs=[pl.BlockSpec((None, TILE, D_MODEL), lambda b, s: (b, s, 0))]
        + [_full(a.shape) for a in mla_args] + [tab_spec, tab_spec]
        + [_full(a.shape) for a in fox_args] + [_full(wq.shape), kv_spec, kv_spec],
        out_specs=op_specs + op_specs + [pl.BlockSpec((None, TILE, MEM_W), lambda b, s: (b, s, 0))],
        out_shape=op_shapes + op_shapes + [jax.ShapeDtypeStruct((bsz, seq, MEM_W), BF16)],
        scratch_shapes=[pltpu.VMEM((1, 128), F32)],
        compiler_params=_cparams(("parallel", "arbitrary")),
        name="mixer_prep",
    )(h, *mla_args, cos_t, sin_t, *fox_args, wq, mk, mv)
    return outs[0:3], outs[3:6], outs[6]


def _merge_kernel(h_ref, omla_ref, ofox_ref, omem_ref, wg_ref, bg_ref, wbm_ref, wbf_ref, wbc_ref,
                  wo_ref, g_ref, b_ref, o_ref):
    o_mla = omla_ref[...].reshape(MLA_W, TILE)
    o_fox = ofox_ref[...].reshape(FOX_W, TILE)
    rows = TILE // MERGE_ROW_GROUPS
    for r in range(MERGE_ROW_GROUPS):
        sl = slice(r * rows, (r + 1) * rows)
        h = h_ref[sl, :]
        gates = jax.nn.sigmoid(_dot(h.astype(BF16), wg_ref[...]) + bg_ref[...])
        a_mla = _tn(o_mla[:, sl], wbm_ref[...])
        a_fox = _tn(o_fox[:, sl], wbf_ref[...])
        a_mem = _dot(omem_ref[sl, :], wbc_ref[...])
        merged = (gates[:, :D_MODEL] * a_mla + gates[:, D_MODEL:2 * D_MODEL] * a_fox
                  + gates[:, 2 * D_MODEL:] * a_mem)
        mix = _dot(merged.astype(BF16), wo_ref[...])
        o_ref[sl, :] = _layer_norm(ALPHA * h + mix, g_ref[...], b_ref[...])


def _merge(h, o_mla, o_fox, o_mem, w):
    bsz, seq, _ = h.shape
    ot_spec = pl.BlockSpec((None, HEADS, None, 64, TILE), lambda b, s: (b, 0, s, 0, 0))
    names = ["wg", "bg", "wbm", "wbf", "wbc", "wo", "g", "b"]
    return pl.pallas_call(
        _merge_kernel,
        grid=(bsz, seq // TILE),
        in_specs=[pl.BlockSpec((None, TILE, D_MODEL), lambda b, s: (b, s, 0)), ot_spec, ot_spec,
                  pl.BlockSpec((None, TILE, MEM_W), lambda b, s: (b, s, 0))]
        + [_full(w[n].shape) for n in names],
        out_specs=pl.BlockSpec((None, TILE, D_MODEL), lambda b, s: (b, s, 0)),
        out_shape=jax.ShapeDtypeStruct((bsz, seq, D_MODEL), F32),
        compiler_params=_cparams(("parallel", "parallel")),
        name="merge",
    )(h, o_mla, o_fox, o_mem, *[w[n] for n in names])


def _placement_matrices():
    pk = np.zeros((128, HEADS * QK_PAD), np.float32)
    pq = np.zeros((HEADS * 16, 128), np.float32)
    for h in range(HEADS):
        for part in range(3):
            pk[8 * part + h, QK_PAD * h + 64 + part] = -1.0
            pk[24, QK_PAD * h + 67 + part] = 1.0
            pq[16 * h + part, 24] = 1.0
            pq[16 * h + 3 + part, 8 * part + h] = 1.0
    tri = np.tril(np.ones((TILE, TILE), np.float32))
    return jnp.asarray(pk, BF16), jnp.asarray(pq, BF16), jnp.asarray(tri, BF16)


def _pad_heads(w, width):
    k = w.shape[0]
    w = w.reshape(k, HEADS, width)
    return jnp.pad(w, ((0, 0), (0, 0), (0, QK_PAD - width))).reshape(k, HEADS * QK_PAD)


def _swap_halves(w, axis):
    a, b = jnp.split(w, 2, axis=axis)
    return jnp.concatenate([b, a], axis=axis)


def kernel(x, mem, positions, ln1_g, ln1_b, ffn1_w_in, ffn1_w_down, w_in, b_gate, mla_q_norm, mla_w_uq, mla_kv_norm, mla_w_ukv, fox_b_f, mem_w_kv, w_br_mla, w_br_fox, w_br_mem, w_out, ln2_g, ln2_b, ffn2_w_in, ffn2_w_down, ln3_g, ln3_b):
    bsz, seq, _ = x.shape
    n = bsz * seq
    row = lambda v: v.reshape(1, -1).astype(F32)
    bf = lambda v: v.astype(BF16)

    sizes = (MLA_Q_LORA, MLA_KV_LORA, MLA_ROPE, FOX_W, FOX_W, FOX_W, FOX_HEADS, MEM_W)
    offs = np.cumsum((0,) + sizes).tolist()
    wi = w_in[0]
    w_cq, w_ckv, w_kpe, w_fq, w_fk, w_fv, w_fl, w_mq = (wi[:, offs[i]:offs[i + 1]] for i in range(8))
    w_gates = wi[:, offs[8]:]

    uq = mla_w_uq[0].reshape(MLA_Q_LORA, MLA_HEADS, MLA_NOPE + MLA_ROPE)
    uq_rope_sw = _swap_halves(uq[:, :, MLA_NOPE:], axis=2).reshape(MLA_Q_LORA, MLA_HEADS * MLA_ROPE)
    ukv = mla_w_ukv[0].reshape(MLA_KV_LORA, MLA_HEADS, MLA_NOPE + MLA_V)
    w_kpe_t = jnp.concatenate([w_kpe.T, _swap_halves(w_kpe, axis=1).T,
                               jnp.zeros((QK_PAD - 2 * MLA_ROPE, D_MODEL), F32)], axis=0)
    mla_w = {
        "wc": bf(jnp.concatenate([w_cq, w_ckv], axis=1)),
        "qg": row(mla_q_norm[0]), "kvg": row(mla_kv_norm[0]),
        "wuq": bf(mla_w_uq[0].T), "wuqsw": bf(uq_rope_sw.T), "wkpe": bf(w_kpe_t),
        "wk": bf(_pad_heads(ukv[:, :, :MLA_NOPE].reshape(MLA_KV_LORA, -1), MLA_NOPE)),
        "wv": bf(ukv[:, :, MLA_NOPE:].reshape(MLA_KV_LORA, -1).T),
    }

    pk, pq, tri = _placement_matrices()
    w_fl3 = jnp.concatenate([w_fl, w_fl, w_fl, jnp.zeros((D_MODEL, 128 - 3 * FOX_HEADS), F32)], axis=1)
    b_fl3 = jnp.concatenate([fox_b_f[0]] * 3 + [jnp.zeros((128 - 3 * FOX_HEADS,), F32)]).reshape(1, 128)
    fox_w = {
        "wk": bf(_pad_heads(w_fk, FOX_DIM)), "wfl": bf(w_fl3), "bfl": b_fl3,
        "wq": bf(w_fq.T), "wv": bf(w_fv.T), "tri": tri, "pk": pk, "pq": pq,
    }

    merge_w = {
        "wg": bf(w_gates), "bg": row(b_gate[0]), "wbm": bf(w_br_mla[0]), "wbf": bf(w_br_fox[0]),
        "wbc": bf(w_br_mem[0]), "wo": bf(w_out[0]), "g": row(ln2_g[0]), "b": row(ln2_b[0]),
    }

    h1 = _ffn_ln(x.reshape(n, D_MODEL), bf(ffn1_w_in[0]), bf(ffn1_w_down[0]), row(ln1_g[0]), row(ln1_b[0]))
    h1 = h1.reshape(bsz, seq, D_MODEL)

    cos_t, sin_t = _rope_tables(positions)
    mk, mv = _mem_kv(mem, bf(mem_w_kv[0]))
    mla_ops, fox_ops, o_mem = _mixer_prep(h1, mla_w, cos_t, sin_t, fox_w, bf(w_mq), mk, mv)
    o_mla = _attention(*mla_ops)
    o_fox = _attention(*fox_ops)

    h2 = _merge(h1, o_mla, o_fox, o_mem, merge_w)
    out = _ffn_ln(h2.reshape(n, D_MODEL), bf(ffn2_w_in[0]), bf(ffn2_w_down[0]), row(ln3_g[0]), row(ln3_b[0]))
    return out.reshape(bsz, seq, D_MODEL)
```

```python
import functools

import numpy as np
import jax
import jax.numpy as jnp
from jax import lax
from jax.experimental import pallas as pl
from jax.experimental.pallas import tpu as pltpu

F32 = jnp.float32
BF16 = jnp.bfloat16

D_MODEL = 1024
MEM_LEN = 256
MLA_HEADS = 8
MLA_NOPE = 64
MLA_ROPE = 32
MLA_V = 64
MLA_Q_LORA = 384
MLA_KV_LORA = 256
FOX_HEADS = 8
FOX_DIM = 64
MEM_HEADS = 4
MEM_DIM = 128
D_FF = 2816
ROPE_THETA = 10000.0
LN_EPS = 1e-5
RMS_EPS = 1e-6
DEPTH = 1
ALPHA = (2.0 * DEPTH) ** 0.25
MLA_W = MLA_HEADS * MLA_V
FOX_W = FOX_HEADS * FOX_DIM
MEM_W = MEM_HEADS * MEM_DIM

HEADS = 8
QK_PAD = 128
V_ROWS = 80
TILE = 512
FF_ROW_GROUPS = 2
MERGE_ROW_GROUPS = 2
VMEM_LIMIT = 56 * 1024 * 1024
LOG2E = 1.4426950408889634


def _cparams(sem):
    return pltpu.CompilerParams(dimension_semantics=sem, vmem_limit_bytes=VMEM_LIMIT)


def _nt(a, b):
    return lax.dot_general(a, b, (((1,), (1,)), ((), ())), preferred_element_type=F32)


def _tn(a, b):
    return lax.dot_general(a, b, (((0,), (0,)), ((), ())), preferred_element_type=F32)


def _dot(a, b):
    return jnp.dot(a, b, preferred_element_type=F32)


def _layer_norm(y, g, b):
    mu = jnp.mean(y, axis=-1, keepdims=True)
    d = y - mu
    var = jnp.mean(d * d, axis=-1, keepdims=True)
    return d * lax.rsqrt(var + LN_EPS) * g + b


def _rms_norm(x, g):
    return x * lax.rsqrt(jnp.mean(x * x, axis=-1, keepdims=True) + RMS_EPS) * g


def _split3(x):
    hi = x.astype(BF16).astype(F32)
    r = x - hi
    mid = r.astype(BF16).astype(F32)
    lo = (r - mid).astype(BF16).astype(F32)
    return hi, mid, lo


def _ffn_ln_kernel(x_ref, wa_ref, wb_ref, wd_ref, g_ref, b_ref, o_ref):
    rows = TILE // FF_ROW_GROUPS
    for r in range(FF_ROW_GROUPS):
        sl = slice(r * rows, (r + 1) * rows)
        x = x_ref[sl, :]
        xb = x.astype(BF16)
        a = _dot(xb, wa_ref[...])
        b = _dot(xb, wb_ref[...])
        gated = (a * jax.nn.sigmoid(a) * b).astype(BF16)
        y = ALPHA * x + 0.5 * _dot(gated, wd_ref[...])
        o_ref[sl, :] = _layer_norm(y, g_ref[...], b_ref[...])


def _resident(shape, index_map):
    return pl.BlockSpec(shape, index_map, pipeline_mode=pl.Buffered(1))


def _ffn_ln(x, w_in, w_down, g, b):
    n = x.shape[0]
    return pl.pallas_call(
        _ffn_ln_kernel,
        grid=(n // TILE,),
        in_specs=[
            pl.BlockSpec((TILE, D_MODEL), lambda i: (i, 0)),
            _resident((D_MODEL, D_FF), lambda i: (0, 0)),
            _resident((D_MODEL, D_FF), lambda i: (0, 1)),
            _resident((D_FF, D_MODEL), lambda i: (0, 0)),
            _resident((1, D_MODEL), lambda i: (0, 0)),
            _resident((1, D_MODEL), lambda i: (0, 0)),
        ],
        out_specs=pl.BlockSpec((TILE, D_MODEL), lambda i: (i, 0)),
        out_shape=jax.ShapeDtypeStruct((n, D_MODEL), F32),
        compiler_params=_cparams(("parallel",)),
        name="ffn_ln",
    )(x, w_in, w_in, w_down, g, b)


def _rope_kernel(pos_ref, inv_ref, cos_ref, sin_ref):
    ang = pos_ref[...].astype(F32) * inv_ref[...]
    c = jnp.cos(ang)
    s = jnp.sin(ang)
    cos_ref[...] = jnp.concatenate([c, c], axis=0)
    sin_ref[...] = jnp.concatenate([-s, s], axis=0)


def _rope_tables(positions):
    bsz, seq = positions.shape
    half = MLA_ROPE // 2
    inv_freq = (ROPE_THETA ** (-jnp.arange(half, dtype=F32) / half)).reshape(half, 1)
    ts = min(seq, 2048)
    spec = pl.BlockSpec((None, MLA_ROPE, ts), lambda b, s: (b, 0, s))
    return pl.pallas_call(
        _rope_kernel,
        grid=(bsz, seq // ts),
        in_specs=[pl.BlockSpec((None, 1, ts), lambda b, s: (b, 0, s)),
                  pl.BlockSpec((half, 1), lambda b, s: (0, 0))],
        out_specs=[spec, spec],
        out_shape=[jax.ShapeDtypeStruct((bsz, MLA_ROPE, seq), F32)] * 2,
        compiler_params=_cparams(("parallel", "parallel")),
        name="rope",
    )(positions.reshape(bsz, 1, seq), inv_freq)


def _attn_operand_specs(bsz, seq):
    nt = seq // TILE
    qt_spec = pl.BlockSpec((None, HEADS, None, QK_PAD, TILE), lambda b, s: (b, 0, s, 0, 0))
    k_spec = pl.BlockSpec((None, TILE, HEADS * QK_PAD), lambda b, s: (b, s, 0))
    vt_spec = pl.BlockSpec((None, HEADS, None, V_ROWS, TILE), lambda b, s: (b, 0, s, 0, 0))
    shapes = [jax.ShapeDtypeStruct((bsz, HEADS, nt, QK_PAD, TILE), BF16),
              jax.ShapeDtypeStruct((bsz, seq, HEADS * QK_PAD), BF16),
              jax.ShapeDtypeStruct((bsz, HEADS, nt, V_ROWS, TILE), BF16)]
    return [qt_spec, k_spec, vt_spec], shapes


def _store_vt(vt_ref, vt):
    row = lax.broadcasted_iota(jnp.int32, (V_ROWS - 64, TILE), 0)
    tail = jnp.where(row == 0, 1.0, 0.0).astype(BF16)
    for h in range(HEADS):
        vt_ref[h, 0:64, :] = vt[64 * h:64 * h + 64].astype(BF16)
        vt_ref[h, 64:V_ROWS, :] = tail


def _full(shape):
    return pl.BlockSpec(shape, lambda b, s: (0,) * len(shape))


def _mla_operands(hb, c, qg_ref, kvg_ref, wuq_ref, wuqsw_ref, wkpe_ref, wk_ref, wv_ref,
                  cos_ref, sin_ref, qt_ref, k_ref, vt_ref):
    scale = (MLA_NOPE + MLA_ROPE) ** -0.5 * LOG2E
    cqn = _rms_norm(c[:, :MLA_Q_LORA], qg_ref[...]).astype(BF16)
    kvn = _rms_norm(c[:, MLA_Q_LORA:MLA_Q_LORA + MLA_KV_LORA], kvg_ref[...]).astype(BF16)
    cos_t = cos_ref[...]
    sin_t = sin_ref[...]

    q_t = _nt(wuq_ref[...], cqn)
    qsw_t = _nt(wuqsw_ref[...], cqn)
    dq = MLA_NOPE + MLA_ROPE
    pad = jnp.zeros((QK_PAD - dq, TILE), BF16)
    for h in range(HEADS):
        nope = (q_t[dq * h:dq * h + MLA_NOPE] * scale).astype(BF16)
        rope = (q_t[dq * h + MLA_NOPE:dq * (h + 1)] * cos_t
                + qsw_t[MLA_ROPE * h:MLA_ROPE * (h + 1)] * sin_t)
        rope = (rope * scale).astype(BF16)
        if h % 2 == 0:
            qt_ref[h, 0:MLA_NOPE, :] = nope
            qt_ref[h, MLA_NOPE:dq, :] = rope
            qt_ref[h, dq:QK_PAD, :] = pad
        else:
            qt_ref[h, 0:MLA_ROPE, :] = rope
            qt_ref[h, MLA_ROPE:QK_PAD - MLA_NOPE, :] = pad
            qt_ref[h, QK_PAD - MLA_NOPE:QK_PAD, :] = nope

    kpe_t = _nt(wkpe_ref[...], hb)
    kr_t = kpe_t[0:MLA_ROPE] * cos_t + kpe_t[MLA_ROPE:2 * MLA_ROPE] * sin_t
    gap = jnp.zeros((MLA_NOPE - MLA_ROPE, TILE), F32)
    kr = jnp.concatenate([kr_t, gap, kr_t, gap], axis=0).T
    k_nope = _dot(kvn, wk_ref[...])
    lane = lax.broadcasted_iota(jnp.int32, (TILE, 128), 1)
    for g in range(HEADS // 2):
        pair = k_nope[:, 128 * g:128 * (g + 1)]
        k_ref[:, 256 * g:256 * g + 128] = jnp.where(lane < 64, pair, kr).astype(BF16)
        k_ref[:, 256 * g + 128:256 * (g + 1)] = jnp.where(lane < 64, kr, pair).astype(BF16)

    _store_vt(vt_ref, _nt(wv_ref[...], kvn))


MLA_WEIGHT_NAMES = ("qg", "kvg", "wuq", "wuqsw", "wkpe", "wk", "wv")
FOX_WEIGHT_NAMES = ("wk", "bfl", "wq", "wv", "tri", "pk", "pq")


def _fox_operands(hb, z, wk_ref, bfl_ref, wq_ref, wv_ref, tri_ref, pk_ref, pq_ref,
                  qt_ref, k_ref, vt_ref, carry_ref):
    @pl.when(pl.program_id(1) == 0)
    def _():
        carry_ref[...] = jnp.zeros_like(carry_ref)

    lane = lax.broadcasted_iota(jnp.int32, (TILE, 128), 1)
    z = z + bfl_ref[...]
    log_f = jnp.minimum(z, 0.0) - jnp.log1p(jnp.exp(-jnp.abs(z)))
    hi, mid, lo = _split3(log_f)
    parts = jnp.where(lane < 8, hi, jnp.where(lane < 16, mid, jnp.where(lane < 24, lo, 0.0)))
    cum = _dot(tri_ref[...], parts.astype(BF16))
    f = cum + pltpu.roll(cum, 120, 1) + pltpu.roll(cum, 112, 1) + carry_ref[...]
    f = jnp.where(lane < 8, f, 0.0)
    carry_ref[...] = f[TILE - 1:TILE, :]

    fhi, fmid, flo = _split3(f * LOG2E)
    p = (fhi + pltpu.roll(fmid, 8, 1) + pltpu.roll(flo, 16, 1)
         + jnp.where(lane == 24, 1.0, 0.0)).astype(BF16)
    fk = _dot(hb, wk_ref[...])
    aug = _dot(p, pk_ref[...])
    aug_t = _nt(pq_ref[...], p)
    q_t = _nt(wq_ref[...], hb)
    scale = FOX_DIM ** -0.5 * LOG2E
    for g in range(HEADS // 2):
        fk_pair = fk[:, 128 * g:128 * (g + 1)]
        aug_pair = aug[:, 128 * g:128 * (g + 1)]
        k_ref[:, 256 * g:256 * g + 128] = jnp.where(lane < 64, fk_pair, aug_pair).astype(BF16)
        k_ref[:, 256 * g + 128:256 * (g + 1)] = jnp.where(lane < 64, aug_pair, fk_pair).astype(BF16)
        he, ho = 2 * g, 2 * g + 1
        qt_ref[he, 0:64, :] = (q_t[64 * he:64 * he + 64] * scale).astype(BF16)
        qt_ref[he, 64:80, :] = aug_t[16 * he:16 * he + 16].astype(BF16)
        qt_ref[he, 80:QK_PAD, :] = jnp.zeros((QK_PAD - 80, TILE), BF16)
        qt_ref[ho, 0:16, :] = aug_t[16 * ho:16 * ho + 16].astype(BF16)
        qt_ref[ho, 16:64, :] = jnp.zeros((48, TILE), BF16)
        qt_ref[ho, 64:QK_PAD, :] = (q_t[64 * ho:64 * ho + 64] * scale).astype(BF16)

    _store_vt(vt_ref, _nt(wv_ref[...], hb))


MAX_LAG = 16.0
MAX_OVERSHOOT = 96.0
FAST_CHUNKS_PER_STEP = 7


def _attn_kernel(qt_ref, k_ref, vt_ref, o_ref, s0_ref, s1_ref, s2_ref, s3_ref, cm_ref, m_ref, acc_ref,
                 flag_ref, *, n_tiles):
    t = TILE
    nq = n_tiles // 2
    row = lax.broadcasted_iota(jnp.int32, (t, t), 0)
    col = lax.broadcasted_iota(jnp.int32, (t, t), 1)

    def logits(kblk, qtile, half, s_ref, slot):
        k = k_ref[pl.ds(pl.multiple_of(kblk * t, t), t), :]
        s = _dot(k, qt_ref[2 * qtile + half])
        s_ref[:, half * t:(half + 1) * t] = s
        cm_ref[slot, :, half * t:(half + 1) * t] = jnp.max(s, axis=0, keepdims=True)

    def softmax_pv(kblk, qtile, half, s_ref, slot, diagonal=False, first=False):
        cols = slice(half * t, (half + 1) * t)
        s = s_ref[:, cols]
        if diagonal:
            s = jnp.where(row > col, -jnp.inf, s)
            cm = jnp.max(s, axis=0, keepdims=True)
        else:
            cm = cm_ref[slot, :, cols]
        m_old = cm if first else m_ref[qtile, :, cols]
        m_new = cm if first else jnp.maximum(m_old, cm)
        p = jnp.exp2(s - m_new).astype(BF16)
        pv = _dot(vt_ref[kblk], p)
        if first:
            acc_ref[qtile, :, cols] = pv
        else:
            acc_ref[qtile, :, cols] = jnp.exp2(m_old - m_new) * acc_ref[qtile, :, cols] + pv
        m_ref[qtile, :, cols] = m_new

    def diag_logits(q, sa_ref, sb_ref, slot):
        logits(2 * q, q, 0, sa_ref, slot)
        logits(2 * q, q, 1, sa_ref, slot)
        logits(2 * q + 1, q, 1, sb_ref, slot + 1)

    def diag_softmax_pv(q, sa_ref, sb_ref, slot):
        softmax_pv(2 * q, q, 0, sa_ref, slot, diagonal=True, first=True)
        softmax_pv(2 * q, q, 1, sa_ref, slot, first=True)
        softmax_pv(2 * q + 1, q, 1, sb_ref, slot + 1, diagonal=True)

    def diagonal_phase():
        diag_logits(0, s0_ref, s1_ref, 0)

        def diag_pair(i, c):
            qa, qb = 2 * i, 2 * i + 1
            diag_logits(qb, s2_ref, s3_ref, 2)
            diag_softmax_pv(qa, s0_ref, s1_ref, 0)
            diag_logits(jnp.minimum(qa + 2, nq - 1), s0_ref, s1_ref, 0)
            diag_softmax_pv(qb, s2_ref, s3_ref, 2)
            return c

        lax.fori_loop(0, nq // 2, diag_pair, 0)

    def advance(q, j):
        wrap = j + 1 >= q
        return jnp.where(wrap, jnp.minimum(q + 1, nq - 1), q), jnp.where(wrap, 0, j + 1)

    n_chunks = nq * (nq - 1) // 2
    first_chunk = (jnp.int32(1), jnp.int32(0))

    def fast_unit(q, j, half):
        cols = slice(half * t, (half + 1) * t)
        k = k_ref[pl.ds(pl.multiple_of(j * 2 * t, 2 * t), 2 * t), :]
        s = _dot(k, qt_ref[2 * q + half])
        cm = jnp.max(s, axis=0, keepdims=True)
        m_old = m_ref[q, :, cols]
        p = jnp.exp2(s - m_old).astype(BF16)
        vt = jnp.concatenate([vt_ref[2 * j], vt_ref[2 * j + 1]], axis=1)
        pv = _dot(vt, p)
        m_new = jnp.maximum(m_old, cm - MAX_LAG)
        acc_ref[q, :, cols] = (acc_ref[q, :, cols] + pv) * jnp.exp2(m_old - m_new)
        m_ref[q, :, cols] = m_new
        flag_ref[:, cols] = jnp.maximum(flag_ref[:, cols], cm - m_old)

    def fast_phase():
        def step(i, c):
            for _ in range(FAST_CHUNKS_PER_STEP):
                for half in range(2):
                    fast_unit(c[0], c[1], half)
                c = advance(*c)
            return c

        c = lax.fori_loop(0, n_chunks // FAST_CHUNKS_PER_STEP, step, first_chunk)
        for _ in range(n_chunks % FAST_CHUNKS_PER_STEP):
            for half in range(2):
                fast_unit(c[0], c[1], half)
            c = advance(*c)

    def safe_chunk(q, j, qn, jn):
        for half in range(2):
            logits(2 * j + 1, q, half, s1_ref, 1)
        for half in range(2):
            softmax_pv(2 * j, q, half, s0_ref, 0)
        for half in range(2):
            logits(2 * jn, qn, half, s0_ref, 0)
        for half in range(2):
            softmax_pv(2 * j + 1, q, half, s1_ref, 1)

    def safe_phase():
        for half in range(2):
            logits(0, 1, half, s0_ref, 0)

        def step(i, c):
            nxt = advance(*c)
            safe_chunk(*c, *nxt)
            return nxt

        lax.fori_loop(0, n_chunks, step, first_chunk)

    def finalize():
        def body(q, c):
            for half in range(2):
                cols = slice(half * t, (half + 1) * t)
                o_ref[2 * q + half] = (acc_ref[q, 0:64, cols] / acc_ref[q, 64:65, cols]).astype(BF16)
            return c

        lax.fori_loop(0, nq, body, 0)

    flag_ref[...] = jnp.full(flag_ref.shape, -jnp.inf, F32)
    diagonal_phase()
    if n_chunks:
        fast_phase()
    finalize()

    @pl.when(jnp.max(flag_ref[...]) > MAX_OVERSHOOT)
    def _():
        diagonal_phase()
        if n_chunks:
            safe_phase()
        finalize()


def _attention(qt, k, vt):
    bsz, heads, nt = qt.shape[:3]
    seq = k.shape[1]
    return pl.pallas_call(
        functools.partial(_attn_kernel, n_tiles=nt),
        grid=(bsz, heads),
        in_specs=[pl.BlockSpec((None, None, nt, QK_PAD, TILE), lambda b, h: (b, h, 0, 0, 0)),
                  pl.BlockSpec((None, seq, QK_PAD), lambda b, h: (b, 0, h)),
                  pl.BlockSpec((None, None, nt, V_ROWS, TILE), lambda b, h: (b, h, 0, 0, 0))],
        out_specs=pl.BlockSpec((None, None, nt, 64, TILE), lambda b, h: (b, h, 0, 0, 0)),
        out_shape=jax.ShapeDtypeStruct((bsz, heads, nt, 64, TILE), BF16),
        scratch_shapes=[pltpu.VMEM((TILE, 2 * TILE), F32)] * 4
        + [pltpu.VMEM((4, 1, 2 * TILE), F32), pltpu.VMEM((nt // 2, 1, 2 * TILE), F32),
           pltpu.VMEM((nt // 2, V_ROWS, 2 * TILE), F32), pltpu.VMEM((1, 2 * TILE), F32)],
        compiler_params=_cparams(("parallel", "parallel")),
        name="attn",
    )(qt, k, vt)


def _mem_kv_kernel(mem_ref, w_ref, mk_ref, mv_ref):
    kv = _dot(mem_ref[...].astype(BF16), w_ref[...])
    mk_ref[...] = kv[:, :MEM_W].astype(BF16)
    mv_ref[...] = kv[:, MEM_W:].astype(BF16)


def _mem_kv(mem, w):
    bsz = mem.shape[0]
    spec = pl.BlockSpec((None, MEM_LEN, MEM_W), lambda b: (b, 0, 0))
    return pl.pallas_call(
        _mem_kv_kernel,
        grid=(bsz,),
        in_specs=[pl.BlockSpec((None, MEM_LEN, D_MODEL), lambda b: (b, 0, 0)),
                  pl.BlockSpec(w.shape, lambda b: (0, 0))],
        out_specs=[spec, spec],
        out_shape=[jax.ShapeDtypeStruct((bsz, MEM_LEN, MEM_W), BF16)] * 2,
        compiler_params=_cparams(("parallel",)),
        name="mem_kv",
    )(mem, w)


def _mem_branch(hb, wq_ref, mk_ref, mv_ref, o_ref):
    scale = MEM_DIM ** -0.5
    mq = _dot(hb, wq_ref[...]).astype(BF16)
    for h in range(MEM_HEADS):
        sl = slice(MEM_DIM * h, MEM_DIM * (h + 1))
        s = _nt(mq[:, sl], mk_ref[:, sl]) * scale
        e = jnp.exp(s - jnp.max(s, axis=-1, keepdims=True))
        o = _dot(e.astype(BF16), mv_ref[:, sl]) / jnp.sum(e, axis=-1, keepdims=True)
        o_ref[:, sl] = o.astype(BF16)


def _mixer_prep_kernel(*refs):
    n_mla, n_fox = len(MLA_WEIGHT_NAMES), len(FOX_WEIGHT_NAMES)
    refs = list(refs)
    h_ref, wc_ref = refs.pop(0), refs.pop(0)
    mla_in = [refs.pop(0) for _ in range(n_mla + 2)]
    fox_in = [refs.pop(0) for _ in range(n_fox)]
    mem_in = [refs.pop(0) for _ in range(3)]
    mla_out = [refs.pop(0) for _ in range(3)]
    fox_out = [refs.pop(0) for _ in range(3)]
    omem_ref, carry_ref = refs
    hb = h_ref[...].astype(BF16)
    c = _dot(hb, wc_ref[...])
    _mla_operands(hb, c, *mla_in, *mla_out)
    _fox_operands(hb, c[:, MLA_Q_LORA + MLA_KV_LORA:], *fox_in, *fox_out, carry_ref)
    _mem_branch(hb, *mem_in, omem_ref)


def _mixer_prep(h, wc, mla_w, cos_t, sin_t, fox_w, wq, mk, mv):
    bsz, seq, _ = h.shape
    op_specs, op_shapes = _attn_operand_specs(bsz, seq)
    tab_spec = pl.BlockSpec((None, MLA_ROPE, TILE), lambda b, s: (b, 0, s))
    kv_spec = pl.BlockSpec((None, MEM_LEN, MEM_W), lambda b, s: (b, 0, 0))
    mla_args = [mla_w[n] for n in MLA_WEIGHT_NAMES]
    fox_args = [fox_w[n] for n in FOX_WEIGHT_NAMES]
    outs = pl.pallas_call(
        _mixer_prep_kernel,
        grid=(bsz, seq // TILE),
        in_specs=[pl.BlockSpec((None, TILE, D_MODEL), lambda b, s: (b, s, 0)), _full(wc.shape)]
        + [_full(a.shape) for a in mla_args] + [tab_spec, tab_spec]
        + [_full(a.shape) for a in fox_args] + [_full(wq.shape), kv_spec, kv_spec],
        out_specs=op_specs + op_specs + [pl.BlockSpec((None, TILE, MEM_W), lambda b, s: (b, s, 0))],
        out_shape=op_shapes + op_shapes + [jax.ShapeDtypeStruct((bsz, seq, MEM_W), BF16)],
        scratch_shapes=[pltpu.VMEM((1, 128), F32)],
        compiler_params=_cparams(("parallel", "arbitrary")),
        name="mixer_prep",
    )(h, wc, *mla_args, cos_t, sin_t, *fox_args, wq, mk, mv)
    return outs[0:3], outs[3:6], outs[6]


def _merge_kernel(h_ref, omla_ref, ofox_ref, omem_ref, wg_ref, bg_ref, wbm_ref, wbf_ref, wbc_ref,
                  wo_ref, g_ref, b_ref, o_ref):
    o_mla = omla_ref[...].reshape(MLA_W, TILE)
    o_fox = ofox_ref[...].reshape(FOX_W, TILE)
    rows = TILE // MERGE_ROW_GROUPS
    for r in range(MERGE_ROW_GROUPS):
        sl = slice(r * rows, (r + 1) * rows)
        h = h_ref[sl, :]
        gates = jax.nn.sigmoid(_dot(h.astype(BF16), wg_ref[...]) + bg_ref[...])
        a_mla = _tn(o_mla[:, sl], wbm_ref[...])
        a_fox = _tn(o_fox[:, sl], wbf_ref[...])
        a_mem = _dot(omem_ref[sl, :], wbc_ref[...])
        merged = (gates[:, :D_MODEL] * a_mla + gates[:, D_MODEL:2 * D_MODEL] * a_fox
                  + gates[:, 2 * D_MODEL:] * a_mem)
        mix = _dot(merged.astype(BF16), wo_ref[...])
        o_ref[sl, :] = _layer_norm(ALPHA * h + mix, g_ref[...], b_ref[...])


def _merge(h, o_mla, o_fox, o_mem, w):
    bsz, seq, _ = h.shape
    ot_spec = pl.BlockSpec((None, HEADS, None, 64, TILE), lambda b, s: (b, 0, s, 0, 0))
    names = ["wg", "bg", "wbm", "wbf", "wbc", "wo", "g", "b"]
    return pl.pallas_call(
        _merge_kernel,
        grid=(bsz, seq // TILE),
        in_specs=[pl.BlockSpec((None, TILE, D_MODEL), lambda b, s: (b, s, 0)), ot_spec, ot_spec,
                  pl.BlockSpec((None, TILE, MEM_W), lambda b, s: (b, s, 0))]
        + [_full(w[n].shape) for n in names],
        out_specs=pl.BlockSpec((None, TILE, D_MODEL), lambda b, s: (b, s, 0)),
        out_shape=jax.ShapeDtypeStruct((bsz, seq, D_MODEL), F32),
        compiler_params=_cparams(("parallel", "parallel")),
        name="merge",
    )(h, o_mla, o_fox, o_mem, *[w[n] for n in names])


def _placement_matrices():
    pk = np.zeros((128, HEADS * 64), np.float32)
    pq = np.zeros((HEADS * 16, 128), np.float32)
    for h in range(HEADS):
        base = 128 * (h // 2) + (64 if h % 2 == 0 else 0)
        for part in range(3):
            pk[8 * part + h, base + part] = -1.0
            pk[24, base + 3 + part] = 1.0
            pq[16 * h + part, 24] = 1.0
            pq[16 * h + 3 + part, 8 * part + h] = 1.0
    tri = np.tril(np.ones((TILE, TILE), np.float32))
    return jnp.asarray(pk, BF16), jnp.asarray(pq, BF16), jnp.asarray(tri, BF16)


def _swap_halves(w, axis):
    a, b = jnp.split(w, 2, axis=axis)
    return jnp.concatenate([b, a], axis=axis)


def kernel(x, mem, positions, ln1_g, ln1_b, ffn1_w_in, ffn1_w_down, w_in, b_gate, mla_q_norm, mla_w_uq, mla_kv_norm, mla_w_ukv, fox_b_f, mem_w_kv, w_br_mla, w_br_fox, w_br_mem, w_out, ln2_g, ln2_b, ffn2_w_in, ffn2_w_down, ln3_g, ln3_b):
    bsz, seq, _ = x.shape
    n = bsz * seq
    row = lambda v: v.reshape(1, -1).astype(F32)
    bf = lambda v: v.astype(BF16)

    sizes = (MLA_Q_LORA, MLA_KV_LORA, MLA_ROPE, FOX_W, FOX_W, FOX_W, FOX_HEADS, MEM_W)
    offs = np.cumsum((0,) + sizes).tolist()
    wi = w_in[0]
    w_cq, w_ckv, w_kpe, w_fq, w_fk, w_fv, w_fl, w_mq = (wi[:, offs[i]:offs[i + 1]] for i in range(8))
    w_gates = wi[:, offs[8]:]

    uq = mla_w_uq[0].reshape(MLA_Q_LORA, MLA_HEADS, MLA_NOPE + MLA_ROPE)
    uq_rope_sw = _swap_halves(uq[:, :, MLA_NOPE:], axis=2).reshape(MLA_Q_LORA, MLA_HEADS * MLA_ROPE)
    ukv = mla_w_ukv[0].reshape(MLA_KV_LORA, MLA_HEADS, MLA_NOPE + MLA_V)
    w_kpe_t = jnp.concatenate([w_kpe.T, _swap_halves(w_kpe, axis=1).T,
                               jnp.zeros((QK_PAD - 2 * MLA_ROPE, D_MODEL), F32)], axis=0)
    mla_w = {
        "qg": row(mla_q_norm[0]), "kvg": row(mla_kv_norm[0]),
        "wuq": bf(mla_w_uq[0].T), "wuqsw": bf(uq_rope_sw.T), "wkpe": bf(w_kpe_t),
        "wk": bf(ukv[:, :, :MLA_NOPE].reshape(MLA_KV_LORA, -1)),
        "wv": bf(ukv[:, :, MLA_NOPE:].reshape(MLA_KV_LORA, -1).T),
    }

    pk, pq, tri = _placement_matrices()
    w_fl3 = jnp.concatenate([w_fl, w_fl, w_fl, jnp.zeros((D_MODEL, 128 - 3 * FOX_HEADS), F32)], axis=1)
    b_fl3 = jnp.concatenate([fox_b_f[0]] * 3 + [jnp.zeros((128 - 3 * FOX_HEADS,), F32)]).reshape(1, 128)
    w_c = bf(jnp.concatenate([w_cq, w_ckv, w_fl3], axis=1))
    fox_w = {
        "wk": bf(w_fk), "bfl": b_fl3,
        "wq": bf(w_fq.T), "wv": bf(w_fv.T), "tri": tri, "pk": pk, "pq": pq,
    }

    merge_w = {
        "wg": bf(w_gates), "bg": row(b_gate[0]), "wbm": bf(w_br_mla[0]), "wbf": bf(w_br_fox[0]),
        "wbc": bf(w_br_mem[0]), "wo": bf(w_out[0]), "g": row(ln2_g[0]), "b": row(ln2_b[0]),
    }

    h1 = _ffn_ln(x.reshape(n, D_MODEL), bf(ffn1_w_in[0]), bf(ffn1_w_down[0]), row(ln1_g[0]), row(ln1_b[0]))
    h1 = h1.reshape(bsz, seq, D_MODEL)

    cos_t, sin_t = _rope_tables(positions)
    mk, mv = _mem_kv(mem, bf(mem_w_kv[0]))
    mla_ops, fox_ops, o_mem = _mixer_prep(h1, w_c, mla_w, cos_t, sin_t, fox_w, bf(w_mq), mk, mv)
    o_mla = _attention(*mla_ops)
    o_fox = _attention(*fox_ops)

    h2 = _merge(h1, o_mla, o_fox, o_mem, merge_w)
    out = _ffn_ln(h2.reshape(n, D_MODEL), bf(ffn2_w_in[0]), bf(ffn2_w_down[0]), row(ln3_g[0]), row(ln3_b[0]))
    return out.reshape(bsz, seq, D_MODEL)
```

```python
import functools

import numpy as np
import jax
import jax.numpy as jnp
from jax import lax
from jax.experimental import pallas as pl
from jax.experimental.pallas import tpu as pltpu

F32 = jnp.float32
BF16 = jnp.bfloat16

D_MODEL = 1024
MEM_LEN = 256
MLA_HEADS = 8
MLA_NOPE = 64
MLA_ROPE = 32
MLA_V = 64
MLA_Q_LORA = 384
MLA_KV_LORA = 256
FOX_HEADS = 8
FOX_DIM = 64
MEM_HEADS = 4
MEM_DIM = 128
D_FF = 2816
ROPE_THETA = 10000.0
LN_EPS = 1e-5
RMS_EPS = 1e-6
DEPTH = 1
ALPHA = (2.0 * DEPTH) ** 0.25
MLA_W = MLA_HEADS * MLA_V
FOX_W = FOX_HEADS * FOX_DIM
MEM_W = MEM_HEADS * MEM_DIM

HEADS = 8
QK_PAD = 128
V_ROWS = 80
TILE = 512
FF_ROW_GROUPS = 2
MERGE_ROW_GROUPS = 2
VMEM_LIMIT = 56 * 1024 * 1024
LOG2E = 1.4426950408889634


def _cparams(sem):
    return pltpu.CompilerParams(dimension_semantics=sem, vmem_limit_bytes=VMEM_LIMIT)


def _nt(a, b):
    return lax.dot_general(a, b, (((1,), (1,)), ((), ())), preferred_element_type=F32)


def _tn(a, b):
    return lax.dot_general(a, b, (((0,), (0,)), ((), ())), preferred_element_type=F32)


def _dot(a, b):
    return jnp.dot(a, b, preferred_element_type=F32)


def _layer_norm(y, g, b):
    mu = jnp.mean(y, axis=-1, keepdims=True)
    d = y - mu
    var = jnp.mean(d * d, axis=-1, keepdims=True)
    return d * lax.rsqrt(var + LN_EPS) * g + b


def _rms_norm(x, g):
    return x * lax.rsqrt(jnp.mean(x * x, axis=-1, keepdims=True) + RMS_EPS) * g


def _split3(x):
    hi = x.astype(BF16).astype(F32)
    r = x - hi
    mid = r.astype(BF16).astype(F32)
    lo = (r - mid).astype(BF16).astype(F32)
    return hi, mid, lo


def _ffn_ln_kernel(x_ref, wa_ref, wb_ref, wd_ref, g_ref, b_ref, o_ref):
    rows = TILE // FF_ROW_GROUPS
    for r in range(FF_ROW_GROUPS):
        sl = slice(r * rows, (r + 1) * rows)
        x = x_ref[sl, :]
        xb = x.astype(BF16)
        a = _dot(xb, wa_ref[...])
        b = _dot(xb, wb_ref[...])
        gated = (a * jax.nn.sigmoid(a) * b).astype(BF16)
        y = ALPHA * x + 0.5 * _dot(gated, wd_ref[...])
        o_ref[sl, :] = _layer_norm(y, g_ref[...], b_ref[...])


def _resident(shape, index_map):
    return pl.BlockSpec(shape, index_map, pipeline_mode=pl.Buffered(1))


def _ffn_ln(x, w_in, w_down, g, b):
    n = x.shape[0]
    return pl.pallas_call(
        _ffn_ln_kernel,
        grid=(n // TILE,),
        in_specs=[
            pl.BlockSpec((TILE, D_MODEL), lambda i: (i, 0)),
            _resident((D_MODEL, D_FF), lambda i: (0, 0)),
            _resident((D_MODEL, D_FF), lambda i: (0, 1)),
            _resident((D_FF, D_MODEL), lambda i: (0, 0)),
            _resident((1, D_MODEL), lambda i: (0, 0)),
            _resident((1, D_MODEL), lambda i: (0, 0)),
        ],
        out_specs=pl.BlockSpec((TILE, D_MODEL), lambda i: (i, 0)),
        out_shape=jax.ShapeDtypeStruct((n, D_MODEL), F32),
        compiler_params=_cparams(("parallel",)),
        name="ffn_ln",
    )(x, w_in, w_in, w_down, g, b)


def _rope_kernel(pos_ref, inv_ref, cos_ref, sin_ref):
    ang = pos_ref[...].astype(F32) * inv_ref[...]
    c = jnp.cos(ang)
    s = jnp.sin(ang)
    cos_ref[...] = jnp.concatenate([c, c], axis=0)
    sin_ref[...] = jnp.concatenate([-s, s], axis=0)


def _rope_tables(positions):
    bsz, seq = positions.shape
    half = MLA_ROPE // 2
    inv_freq = (ROPE_THETA ** (-jnp.arange(half, dtype=F32) / half)).reshape(half, 1)
    ts = min(seq, 2048)
    spec = pl.BlockSpec((None, MLA_ROPE, ts), lambda b, s: (b, 0, s))
    return pl.pallas_call(
        _rope_kernel,
        grid=(bsz, seq // ts),
        in_specs=[pl.BlockSpec((None, 1, ts), lambda b, s: (b, 0, s)),
                  pl.BlockSpec((half, 1), lambda b, s: (0, 0))],
        out_specs=[spec, spec],
        out_shape=[jax.ShapeDtypeStruct((bsz, MLA_ROPE, seq), F32)] * 2,
        compiler_params=_cparams(("parallel", "parallel")),
        name="rope",
    )(positions.reshape(bsz, 1, seq), inv_freq)


def _attn_operand_specs(bsz, seq):
    nt = seq // TILE
    qt_spec = pl.BlockSpec((None, HEADS, None, QK_PAD, TILE), lambda b, s: (b, 0, s, 0, 0))
    k_spec = pl.BlockSpec((None, TILE, HEADS * QK_PAD), lambda b, s: (b, s, 0))
    vt_spec = pl.BlockSpec((None, HEADS, None, V_ROWS, TILE), lambda b, s: (b, 0, s, 0, 0))
    shapes = [jax.ShapeDtypeStruct((bsz, HEADS, nt, QK_PAD, TILE), BF16),
              jax.ShapeDtypeStruct((bsz, seq, HEADS * QK_PAD), BF16),
              jax.ShapeDtypeStruct((bsz, HEADS, nt, V_ROWS, TILE), BF16)]
    return [qt_spec, k_spec, vt_spec], shapes


def _store_vt(vt_ref, vt):
    row = lax.broadcasted_iota(jnp.int32, (V_ROWS - 64, TILE), 0)
    tail = jnp.where(row == 0, 1.0, 0.0).astype(BF16)
    for h in range(HEADS):
        vt_ref[h, 0:64, :] = vt[64 * h:64 * h + 64].astype(BF16)
        vt_ref[h, 64:V_ROWS, :] = tail


def _full(shape):
    return pl.BlockSpec(shape, lambda b, s: (0,) * len(shape))


def _mla_operands(hb, c, qg_ref, kvg_ref, wuq_ref, wuqsw_ref, wkpe_ref, wk_ref, wv_ref,
                  cos_ref, sin_ref, qt_ref, k_ref, vt_ref):
    scale = (MLA_NOPE + MLA_ROPE) ** -0.5 * LOG2E
    cqn = _rms_norm(c[:, :MLA_Q_LORA], qg_ref[...]).astype(BF16)
    kvn = _rms_norm(c[:, MLA_Q_LORA:MLA_Q_LORA + MLA_KV_LORA], kvg_ref[...]).astype(BF16)
    cos_t = cos_ref[...]
    sin_t = sin_ref[...]

    q_t = _nt(wuq_ref[...], cqn)
    qsw_t = _nt(wuqsw_ref[...], cqn)
    dq = MLA_NOPE + MLA_ROPE
    pad = jnp.zeros((QK_PAD - dq, TILE), BF16)
    for h in range(HEADS):
        nope = (q_t[dq * h:dq * h + MLA_NOPE] * scale).astype(BF16)
        rope = (q_t[dq * h + MLA_NOPE:dq * (h + 1)] * cos_t
                + qsw_t[MLA_ROPE * h:MLA_ROPE * (h + 1)] * sin_t)
        rope = (rope * scale).astype(BF16)
        if h % 2 == 0:
            qt_ref[h, 0:MLA_NOPE, :] = nope
            qt_ref[h, MLA_NOPE:dq, :] = rope
            qt_ref[h, dq:QK_PAD, :] = pad
        else:
            qt_ref[h, 0:MLA_ROPE, :] = rope
            qt_ref[h, MLA_ROPE:QK_PAD - MLA_NOPE, :] = pad
            qt_ref[h, QK_PAD - MLA_NOPE:QK_PAD, :] = nope

    kpe_t = _nt(wkpe_ref[...], hb)
    kr_t = kpe_t[0:MLA_ROPE] * cos_t + kpe_t[MLA_ROPE:2 * MLA_ROPE] * sin_t
    gap = jnp.zeros((MLA_NOPE - MLA_ROPE, TILE), F32)
    kr = jnp.concatenate([kr_t, gap, kr_t, gap], axis=0).T
    k_nope = _dot(kvn, wk_ref[...])
    lane = lax.broadcasted_iota(jnp.int32, (TILE, 128), 1)
    for g in range(HEADS // 2):
        pair = k_nope[:, 128 * g:128 * (g + 1)]
        k_ref[:, 256 * g:256 * g + 128] = jnp.where(lane < 64, pair, kr).astype(BF16)
        k_ref[:, 256 * g + 128:256 * (g + 1)] = jnp.where(lane < 64, kr, pair).astype(BF16)

    _store_vt(vt_ref, _nt(wv_ref[...], kvn))


MLA_WEIGHT_NAMES = ("qg", "kvg", "wuq", "wuqsw", "wkpe", "wk", "wv")
FOX_WEIGHT_NAMES = ("wk", "bfl", "wq", "wv", "tri", "pk", "pq")


def _fox_operands(hb, z, wk_ref, bfl_ref, wq_ref, wv_ref, tri_ref, pk_ref, pq_ref,
                  qt_ref, k_ref, vt_ref, carry_ref):
    @pl.when(pl.program_id(1) == 0)
    def _():
        carry_ref[...] = jnp.zeros_like(carry_ref)

    lane = lax.broadcasted_iota(jnp.int32, (TILE, 128), 1)
    z = z + bfl_ref[...]
    log_f = jnp.minimum(z, 0.0) - jnp.log1p(jnp.exp(-jnp.abs(z)))
    hi, mid, lo = _split3(log_f)
    parts = jnp.where(lane < 8, hi, jnp.where(lane < 16, mid, jnp.where(lane < 24, lo, 0.0)))
    cum = _dot(tri_ref[...], parts.astype(BF16))
    f = cum + pltpu.roll(cum, 120, 1) + pltpu.roll(cum, 112, 1) + carry_ref[...]
    f = jnp.where(lane < 8, f, 0.0)
    carry_ref[...] = f[TILE - 1:TILE, :]

    fhi, fmid, flo = _split3(f * LOG2E)
    p = (fhi + pltpu.roll(fmid, 8, 1) + pltpu.roll(flo, 16, 1)
         + jnp.where(lane == 24, 1.0, 0.0)).astype(BF16)
    fk = _dot(hb, wk_ref[...])
    aug = _dot(p, pk_ref[...])
    aug_t = _nt(pq_ref[...], p)
    q_t = _nt(wq_ref[...], hb)
    scale = FOX_DIM ** -0.5 * LOG2E
    for g in range(HEADS // 2):
        fk_pair = fk[:, 128 * g:128 * (g + 1)]
        aug_pair = aug[:, 128 * g:128 * (g + 1)]
        k_ref[:, 256 * g:256 * g + 128] = jnp.where(lane < 64, fk_pair, aug_pair).astype(BF16)
        k_ref[:, 256 * g + 128:256 * (g + 1)] = jnp.where(lane < 64, aug_pair, fk_pair).astype(BF16)
        he, ho = 2 * g, 2 * g + 1
        qt_ref[he, 0:64, :] = (q_t[64 * he:64 * he + 64] * scale).astype(BF16)
        qt_ref[he, 64:80, :] = aug_t[16 * he:16 * he + 16].astype(BF16)
        qt_ref[he, 80:QK_PAD, :] = jnp.zeros((QK_PAD - 80, TILE), BF16)
        qt_ref[ho, 0:16, :] = aug_t[16 * ho:16 * ho + 16].astype(BF16)
        qt_ref[ho, 16:64, :] = jnp.zeros((48, TILE), BF16)
        qt_ref[ho, 64:QK_PAD, :] = (q_t[64 * ho:64 * ho + 64] * scale).astype(BF16)

    _store_vt(vt_ref, _nt(wv_ref[...], hb))


MAX_LAG = 16.0
MAX_OVERSHOOT = 96.0
FAST_CHUNKS_PER_STEP = 7
FAST_PIECE = 256
FAST_LOOKAHEAD = 3


def _attn_kernel(qt_ref, k_ref, vt_ref, o_ref, s0_ref, s1_ref, s2_ref, s3_ref, cm_ref, m_ref, acc_ref,
                 flag_ref, *, n_tiles):
    t = TILE
    nq = n_tiles // 2
    row = lax.broadcasted_iota(jnp.int32, (t, t), 0)
    col = lax.broadcasted_iota(jnp.int32, (t, t), 1)

    def logits(kblk, qtile, half, s_ref, slot):
        k = k_ref[pl.ds(pl.multiple_of(kblk * t, t), t), :]
        s = _dot(k, qt_ref[2 * qtile + half])
        s_ref[:, half * t:(half + 1) * t] = s
        cm_ref[slot, :, half * t:(half + 1) * t] = jnp.max(s, axis=0, keepdims=True)

    def softmax_pv(kblk, qtile, half, s_ref, slot, diagonal=False, first=False):
        cols = slice(half * t, (half + 1) * t)
        s = s_ref[:, cols]
        if diagonal:
            s = jnp.where(row > col, -jnp.inf, s)
            cm = jnp.max(s, axis=0, keepdims=True)
        else:
            cm = cm_ref[slot, :, cols]
        m_old = cm if first else m_ref[qtile, :, cols]
        m_new = cm if first else jnp.maximum(m_old, cm)
        p = jnp.exp2(s - m_new).astype(BF16)
        pv = _dot(vt_ref[kblk], p)
        if first:
            acc_ref[qtile, :, cols] = pv
        else:
            acc_ref[qtile, :, cols] = jnp.exp2(m_old - m_new) * acc_ref[qtile, :, cols] + pv
        m_ref[qtile, :, cols] = m_new

    def diag_logits(q, sa_ref, sb_ref, slot):
        logits(2 * q, q, 0, sa_ref, slot)
        logits(2 * q, q, 1, sa_ref, slot)
        logits(2 * q + 1, q, 1, sb_ref, slot + 1)

    def diag_softmax_pv(q, sa_ref, sb_ref, slot):
        softmax_pv(2 * q, q, 0, sa_ref, slot, diagonal=True, first=True)
        softmax_pv(2 * q, q, 1, sa_ref, slot, first=True)
        softmax_pv(2 * q + 1, q, 1, sb_ref, slot + 1, diagonal=True)

    def diagonal_phase():
        diag_logits(0, s0_ref, s1_ref, 0)

        def diag_pair(i, c):
            qa, qb = 2 * i, 2 * i + 1
            diag_logits(qb, s2_ref, s3_ref, 2)
            diag_softmax_pv(qa, s0_ref, s1_ref, 0)
            diag_logits(jnp.minimum(qa + 2, nq - 1), s0_ref, s1_ref, 0)
            diag_softmax_pv(qb, s2_ref, s3_ref, 2)
            return c

        lax.fori_loop(0, nq // 2, diag_pair, 0)

    def advance(q, j):
        wrap = j + 1 >= q
        return jnp.where(wrap, jnp.minimum(q + 1, nq - 1), q), jnp.where(wrap, 0, j + 1)

    n_chunks = nq * (nq - 1) // 2
    first_chunk = (jnp.int32(1), jnp.int32(0))

    pieces_per_unit = 2 * t // FAST_PIECE

    def piece_logits(unit, piece):
        q, j, half = unit
        start = pl.multiple_of(j * 2 * t + piece * FAST_PIECE, FAST_PIECE)
        return _dot(k_ref[pl.ds(start, FAST_PIECE), :], qt_ref[2 * q + half])

    def fast_units(units):
        seq = [(u, piece) for u in units for piece in range(pieces_per_unit)]
        ahead = [piece_logits(*seq[i]) for i in range(min(FAST_LOOKAHEAD, len(seq)))]
        for i, (unit, piece) in enumerate(seq):
            q, j, half = unit
            cols = slice(half * t, (half + 1) * t)
            s = ahead.pop(0)
            if i + FAST_LOOKAHEAD < len(seq):
                ahead.append(piece_logits(*seq[i + FAST_LOOKAHEAD]))
            if piece == 0:
                m_old = m_ref[q, :, cols]
                cm, pv = None, None
            pm = jnp.max(s, axis=0, keepdims=True)
            cm = pm if cm is None else jnp.maximum(cm, pm)
            p = jnp.exp2(s - m_old).astype(BF16)
            blk, off = divmod(piece * FAST_PIECE, t)
            d = _dot(vt_ref[2 * j + blk, :, off:off + FAST_PIECE], p)
            pv = d if pv is None else pv + d
            if piece == pieces_per_unit - 1:
                m_new = jnp.maximum(m_old, cm - MAX_LAG)
                acc_ref[q, :, cols] = (acc_ref[q, :, cols] + pv) * jnp.exp2(m_old - m_new)
                m_ref[q, :, cols] = m_new
                flag_ref[:, cols] = jnp.maximum(flag_ref[:, cols], cm - m_old)

    def fast_phase():
        def chunks_from(c, n):
            units = []
            for _ in range(n):
                units += [(c[0], c[1], 0), (c[0], c[1], 1)]
                c = advance(*c)
            return units, c

        def step(i, c):
            units, c = chunks_from(c, FAST_CHUNKS_PER_STEP)
            fast_units(units)
            return c

        c = lax.fori_loop(0, n_chunks // FAST_CHUNKS_PER_STEP, step, first_chunk)
        if n_chunks % FAST_CHUNKS_PER_STEP:
            fast_units(chunks_from(c, n_chunks % FAST_CHUNKS_PER_STEP)[0])

    def safe_chunk(q, j, qn, jn):
        for half in range(2):
            logits(2 * j + 1, q, half, s1_ref, 1)
        for half in range(2):
            softmax_pv(2 * j, q, half, s0_ref, 0)
        for half in range(2):
            logits(2 * jn, qn, half, s0_ref, 0)
        for half in range(2):
            softmax_pv(2 * j + 1, q, half, s1_ref, 1)

    def safe_phase():
        for half in range(2):
            logits(0, 1, half, s0_ref, 0)

        def step(i, c):
            nxt = advance(*c)
            safe_chunk(*c, *nxt)
            return nxt

        lax.fori_loop(0, n_chunks, step, first_chunk)

    def finalize():
        def body(q, c):
            for half in range(2):
                cols = slice(half * t, (half + 1) * t)
                o_ref[2 * q + half] = (acc_ref[q, 0:64, cols] / acc_ref[q, 64:65, cols]).astype(BF16)
            return c

        lax.fori_loop(0, nq, body, 0)

    flag_ref[...] = jnp.full(flag_ref.shape, -jnp.inf, F32)
    diagonal_phase()
    if n_chunks:
        fast_phase()
    finalize()

    @pl.when(jnp.max(flag_ref[...]) > MAX_OVERSHOOT)
    def _():
        diagonal_phase()
        if n_chunks:
            safe_phase()
        finalize()


def _attention(qt, k, vt):
    bsz, heads, nt = qt.shape[:3]
    seq = k.shape[1]
    return pl.pallas_call(
        functools.partial(_attn_kernel, n_tiles=nt),
        grid=(bsz, heads),
        in_specs=[pl.BlockSpec((None, None, nt, QK_PAD, TILE), lambda b, h: (b, h, 0, 0, 0)),
                  pl.BlockSpec((None, seq, QK_PAD), lambda b, h: (b, 0, h)),
                  pl.BlockSpec((None, None, nt, V_ROWS, TILE), lambda b, h: (b, h, 0, 0, 0))],
        out_specs=pl.BlockSpec((None, None, nt, 64, TILE), lambda b, h: (b, h, 0, 0, 0)),
        out_shape=jax.ShapeDtypeStruct((bsz, heads, nt, 64, TILE), BF16),
        scratch_shapes=[pltpu.VMEM((TILE, 2 * TILE), F32)] * 4
        + [pltpu.VMEM((4, 1, 2 * TILE), F32), pltpu.VMEM((nt // 2, 1, 2 * TILE), F32),
           pltpu.VMEM((nt // 2, V_ROWS, 2 * TILE), F32), pltpu.VMEM((1, 2 * TILE), F32)],
        compiler_params=_cparams(("parallel", "parallel")),
        name="attn",
    )(qt, k, vt)


def _mem_kv_kernel(mem_ref, w_ref, mk_ref, mv_ref):
    kv = _dot(mem_ref[...].astype(BF16), w_ref[...])
    mk_ref[...] = kv[:, :MEM_W].astype(BF16)
    mv_ref[...] = kv[:, MEM_W:].astype(BF16)


def _mem_kv(mem, w):
    bsz = mem.shape[0]
    spec = pl.BlockSpec((None, MEM_LEN, MEM_W), lambda b: (b, 0, 0))
    return pl.pallas_call(
        _mem_kv_kernel,
        grid=(bsz,),
        in_specs=[pl.BlockSpec((None, MEM_LEN, D_MODEL), lambda b: (b, 0, 0)),
                  pl.BlockSpec(w.shape, lambda b: (0, 0))],
        out_specs=[spec, spec],
        out_shape=[jax.ShapeDtypeStruct((bsz, MEM_LEN, MEM_W), BF16)] * 2,
        compiler_params=_cparams(("parallel",)),
        name="mem_kv",
    )(mem, w)


def _mem_branch(hb, wq_ref, mk_ref, mv_ref, o_ref):
    scale = MEM_DIM ** -0.5
    mq = _dot(hb, wq_ref[...]).astype(BF16)
    for h in range(MEM_HEADS):
        sl = slice(MEM_DIM * h, MEM_DIM * (h + 1))
        s = _nt(mq[:, sl], mk_ref[:, sl]) * scale
        e = jnp.exp(s - jnp.max(s, axis=-1, keepdims=True))
        o = _dot(e.astype(BF16), mv_ref[:, sl]) / jnp.sum(e, axis=-1, keepdims=True)
        o_ref[:, sl] = o.astype(BF16)


def _mixer_prep_kernel(*refs):
    n_mla, n_fox = len(MLA_WEIGHT_NAMES), len(FOX_WEIGHT_NAMES)
    refs = list(refs)
    h_ref, wc_ref = refs.pop(0), refs.pop(0)
    mla_in = [refs.pop(0) for _ in range(n_mla + 2)]
    fox_in = [refs.pop(0) for _ in range(n_fox)]
    mem_in = [refs.pop(0) for _ in range(3)]
    mla_out = [refs.pop(0) for _ in range(3)]
    fox_out = [refs.pop(0) for _ in range(3)]
    omem_ref, carry_ref = refs
    hb = h_ref[...].astype(BF16)
    c = _dot(hb, wc_ref[...])
    _mla_operands(hb, c, *mla_in, *mla_out)
    _fox_operands(hb, c[:, MLA_Q_LORA + MLA_KV_LORA:], *fox_in, *fox_out, carry_ref)
    _mem_branch(hb, *mem_in, omem_ref)


def _mixer_prep(h, wc, mla_w, cos_t, sin_t, fox_w, wq, mk, mv):
    bsz, seq, _ = h.shape
    op_specs, op_shapes = _attn_operand_specs(bsz, seq)
    tab_spec = pl.BlockSpec((None, MLA_ROPE, TILE), lambda b, s: (b, 0, s))
    kv_spec = pl.BlockSpec((None, MEM_LEN, MEM_W), lambda b, s: (b, 0, 0))
    mla_args = [mla_w[n] for n in MLA_WEIGHT_NAMES]
    fox_args = [fox_w[n] for n in FOX_WEIGHT_NAMES]
    outs = pl.pallas_call(
        _mixer_prep_kernel,
        grid=(bsz, seq // TILE),
        in_specs=[pl.BlockSpec((None, TILE, D_MODEL), lambda b, s: (b, s, 0)), _full(wc.shape)]
        + [_full(a.shape) for a in mla_args] + [tab_spec, tab_spec]
        + [_full(a.shape) for a in fox_args] + [_full(wq.shape), kv_spec, kv_spec],
        out_specs=op_specs + op_specs + [pl.BlockSpec((None, TILE, MEM_W), lambda b, s: (b, s, 0))],
        out_shape=op_shapes + op_shapes + [jax.ShapeDtypeStruct((bsz, seq, MEM_W), BF16)],
        scratch_shapes=[pltpu.VMEM((1, 128), F32)],
        compiler_params=_cparams(("parallel", "arbitrary")),
        name="mixer_prep",
    )(h, wc, *mla_args, cos_t, sin_t, *fox_args, wq, mk, mv)
    return outs[0:3], outs[3:6], outs[6]


def _merge_kernel(h_ref, omla_ref, ofox_ref, omem_ref, wg_ref, bg_ref, wbm_ref, wbf_ref, wbc_ref,
                  wo_ref, g_ref, b_ref, o_ref):
    o_mla = omla_ref[...].reshape(MLA_W, TILE)
    o_fox = ofox_ref[...].reshape(FOX_W, TILE)
    rows = TILE // MERGE_ROW_GROUPS
    for r in range(MERGE_ROW_GROUPS):
        sl = slice(r * rows, (r + 1) * rows)
        h = h_ref[sl, :]
        gates = jax.nn.sigmoid(_dot(h.astype(BF16), wg_ref[...]) + bg_ref[...])
        a_mla = _tn(o_mla[:, sl], wbm_ref[...])
        a_fox = _tn(o_fox[:, sl], wbf_ref[...])
        a_mem = _dot(omem_ref[sl, :], wbc_ref[...])
        merged = (gates[:, :D_MODEL] * a_mla + gates[:, D_MODEL:2 * D_MODEL] * a_fox
                  + gates[:, 2 * D_MODEL:] * a_mem)
        mix = _dot(merged.astype(BF16), wo_ref[...])
        o_ref[sl, :] = _layer_norm(ALPHA * h + mix, g_ref[...], b_ref[...])


def _merge(h, o_mla, o_fox, o_mem, w):
    bsz, seq, _ = h.shape
    ot_spec = pl.BlockSpec((None, HEADS, None, 64, TILE), lambda b, s: (b, 0, s, 0, 0))
    names = ["wg", "bg", "wbm", "wbf", "wbc", "wo", "g", "b"]
    return pl.pallas_call(
        _merge_kernel,
        grid=(bsz, seq // TILE),
        in_specs=[pl.BlockSpec((None, TILE, D_MODEL), lambda b, s: (b, s, 0)), ot_spec, ot_spec,
                  pl.BlockSpec((None, TILE, MEM_W), lambda b, s: (b, s, 0))]
        + [_full(w[n].shape) for n in names],
        out_specs=pl.BlockSpec((None, TILE, D_MODEL), lambda b, s: (b, s, 0)),
        out_shape=jax.ShapeDtypeStruct((bsz, seq, D_MODEL), F32),
        compiler_params=_cparams(("parallel", "parallel")),
        name="merge",
    )(h, o_mla, o_fox, o_mem, *[w[n] for n in names])


def _placement_matrices():
    pk = np.zeros((128, HEADS * 64), np.float32)
    pq = np.zeros((HEADS * 16, 128), np.float32)
    for h in range(HEADS):
        base = 128 * (h // 2) + (64 if h % 2 == 0 else 0)
        for part in range(3):
            pk[8 * part + h, base + part] = -1.0
            pk[24, base + 3 + part] = 1.0
            pq[16 * h + part, 24] = 1.0
            pq[16 * h + 3 + part, 8 * part + h] = 1.0
    tri = np.tril(np.ones((TILE, TILE), np.float32))
    return jnp.asarray(pk, BF16), jnp.asarray(pq, BF16), jnp.asarray(tri, BF16)


def _swap_halves(w, axis):
    a, b = jnp.split(w, 2, axis=axis)
    return jnp.concatenate([b, a], axis=axis)


def kernel(x, mem, positions, ln1_g, ln1_b, ffn1_w_in, ffn1_w_down, w_in, b_gate, mla_q_norm, mla_w_uq, mla_kv_norm, mla_w_ukv, fox_b_f, mem_w_kv, w_br_mla, w_br_fox, w_br_mem, w_out, ln2_g, ln2_b, ffn2_w_in, ffn2_w_down, ln3_g, ln3_b):
    bsz, seq, _ = x.shape
    n = bsz * seq
    row = lambda v: v.reshape(1, -1).astype(F32)
    bf = lambda v: v.astype(BF16)

    sizes = (MLA_Q_LORA, MLA_KV_LORA, MLA_ROPE, FOX_W, FOX_W, FOX_W, FOX_HEADS, MEM_W)
    offs = np.cumsum((0,) + sizes).tolist()
    wi = w_in[0]
    w_cq, w_ckv, w_kpe, w_fq, w_fk, w_fv, w_fl, w_mq = (wi[:, offs[i]:offs[i + 1]] for i in range(8))
    w_gates = wi[:, offs[8]:]

    uq = mla_w_uq[0].reshape(MLA_Q_LORA, MLA_HEADS, MLA_NOPE + MLA_ROPE)
    uq_rope_sw = _swap_halves(uq[:, :, MLA_NOPE:], axis=2).reshape(MLA_Q_LORA, MLA_HEADS * MLA_ROPE)
    ukv = mla_w_ukv[0].reshape(MLA_KV_LORA, MLA_HEADS, MLA_NOPE + MLA_V)
    w_kpe_t = jnp.concatenate([w_kpe.T, _swap_halves(w_kpe, axis=1).T,
                               jnp.zeros((QK_PAD - 2 * MLA_ROPE, D_MODEL), F32)], axis=0)
    mla_w = {
        "qg": row(mla_q_norm[0]), "kvg": row(mla_kv_norm[0]),
        "wuq": bf(mla_w_uq[0].T), "wuqsw": bf(uq_rope_sw.T), "wkpe": bf(w_kpe_t),
        "wk": bf(ukv[:, :, :MLA_NOPE].reshape(MLA_KV_LORA, -1)),
        "wv": bf(ukv[:, :, MLA_NOPE:].reshape(MLA_KV_LORA, -1).T),
    }

    pk, pq, tri = _placement_matrices()
    w_fl3 = jnp.concatenate([w_fl, w_fl, w_fl, jnp.zeros((D_MODEL, 128 - 3 * FOX_HEADS), F32)], axis=1)
    b_fl3 = jnp.concatenate([fox_b_f[0]] * 3 + [jnp.zeros((128 - 3 * FOX_HEADS,), F32)]).reshape(1, 128)
    w_c = bf(jnp.concatenate([w_cq, w_ckv, w_fl3], axis=1))
    fox_w = {
        "wk": bf(w_fk), "bfl": b_fl3,
        "wq": bf(w_fq.T), "wv": bf(w_fv.T), "tri": tri, "pk": pk, "pq": pq,
    }

    merge_w = {
        "wg": bf(w_gates), "bg": row(b_gate[0]), "wbm": bf(w_br_mla[0]), "wbf": bf(w_br_fox[0]),
        "wbc": bf(w_br_mem[0]), "wo": bf(w_out[0]), "g": row(ln2_g[0]), "b": row(ln2_b[0]),
    }

    h1 = _ffn_ln(x.reshape(n, D_MODEL), bf(ffn1_w_in[0]), bf(ffn1_w_down[0]), row(ln1_g[0]), row(ln1_b[0]))
    h1 = h1.reshape(bsz, seq, D_MODEL)

    cos_t, sin_t = _rope_tables(positions)
    mk, mv = _mem_kv(mem, bf(mem_w_kv[0]))
    mla_ops, fox_ops, o_mem = _mixer_prep(h1, w_c, mla_w, cos_t, sin_t, fox_w, bf(w_mq), mk, mv)
    o_mla = _attention(*mla_ops)
    o_fox = _attention(*fox_ops)

    h2 = _merge(h1, o_mla, o_fox, o_mem, merge_w)
    out = _ffn_ln(h2.reshape(n, D_MODEL), bf(ffn2_w_in[0]), bf(ffn2_w_down[0]), row(ln3_g[0]), row(ln3_b[0]))
    return out.reshape(bsz, seq, D_MODEL)
```

```python
import functools

import numpy as np
import jax
import jax.numpy as jnp
from jax import lax
from jax.experimental import pallas as pl
from jax.experimental.pallas import tpu as pltpu

F32 = jnp.float32
BF16 = jnp.bfloat16

D_MODEL = 1024
MEM_LEN = 256
MLA_HEADS = 8
MLA_NOPE = 64
MLA_ROPE = 32
MLA_V = 64
MLA_Q_LORA = 384
MLA_KV_LORA = 256
FOX_HEADS = 8
FOX_DIM = 64
MEM_HEADS = 4
MEM_DIM = 128
D_FF = 2816
ROPE_THETA = 10000.0
LN_EPS = 1e-5
RMS_EPS = 1e-6
DEPTH = 1
ALPHA = (2.0 * DEPTH) ** 0.25
MLA_W = MLA_HEADS * MLA_V
FOX_W = FOX_HEADS * FOX_DIM
MEM_W = MEM_HEADS * MEM_DIM

HEADS = 8
QK_PAD = 128
V_ROWS = 80
TILE = 512
FF_TILE = 1024
FF_ROW_GROUPS = 4
MERGE_TILE = 1024
MERGE_ROW_GROUPS = 4
VMEM_LIMIT = 56 * 1024 * 1024
LOG2E = 1.4426950408889634


def _cparams(sem):
    return pltpu.CompilerParams(dimension_semantics=sem, vmem_limit_bytes=VMEM_LIMIT)


def _nt(a, b):
    return lax.dot_general(a, b, (((1,), (1,)), ((), ())), preferred_element_type=F32)


def _tn(a, b):
    return lax.dot_general(a, b, (((0,), (0,)), ((), ())), preferred_element_type=F32)


def _dot(a, b):
    return jnp.dot(a, b, preferred_element_type=F32)


def _layer_norm(y, g, b):
    mu = jnp.mean(y, axis=-1, keepdims=True)
    d = y - mu
    var = jnp.mean(d * d, axis=-1, keepdims=True)
    return d * lax.rsqrt(var + LN_EPS) * g + b


def _rms_norm(x, g):
    return x * lax.rsqrt(jnp.mean(x * x, axis=-1, keepdims=True) + RMS_EPS) * g


def _split3(x):
    hi = x.astype(BF16).astype(F32)
    r = x - hi
    mid = r.astype(BF16).astype(F32)
    lo = (r - mid).astype(BF16).astype(F32)
    return hi, mid, lo


def _ffn_ln_kernel(x_ref, wa_ref, wb_ref, wd_ref, g_ref, b_ref, o_ref):
    rows = FF_TILE // FF_ROW_GROUPS
    for r in range(FF_ROW_GROUPS):
        sl = slice(r * rows, (r + 1) * rows)
        x = x_ref[sl, :]
        xb = x.astype(BF16)
        a = _dot(xb, wa_ref[...])
        b = _dot(xb, wb_ref[...])
        gated = (a * jax.nn.sigmoid(a) * b).astype(BF16)
        y = ALPHA * x + 0.5 * _dot(gated, wd_ref[...])
        o_ref[sl, :] = _layer_norm(y, g_ref[...], b_ref[...])


def _resident(shape, index_map):
    return pl.BlockSpec(shape, index_map, pipeline_mode=pl.Buffered(1))


def _ffn_ln(x, w_in, w_down, g, b):
    n = x.shape[0]
    return pl.pallas_call(
        _ffn_ln_kernel,
        grid=(n // FF_TILE,),
        in_specs=[
            pl.BlockSpec((FF_TILE, D_MODEL), lambda i: (i, 0)),
            _resident((D_MODEL, D_FF), lambda i: (0, 0)),
            _resident((D_MODEL, D_FF), lambda i: (0, 1)),
            _resident((D_FF, D_MODEL), lambda i: (0, 0)),
            _resident((1, D_MODEL), lambda i: (0, 0)),
            _resident((1, D_MODEL), lambda i: (0, 0)),
        ],
        out_specs=pl.BlockSpec((FF_TILE, D_MODEL), lambda i: (i, 0)),
        out_shape=jax.ShapeDtypeStruct((n, D_MODEL), F32),
        compiler_params=_cparams(("parallel",)),
        name="ffn_ln",
    )(x, w_in, w_in, w_down, g, b)


def _rope_kernel(pos_ref, inv_ref, cos_ref, sin_ref):
    ang = pos_ref[...].astype(F32) * inv_ref[...]
    c = jnp.cos(ang)
    s = jnp.sin(ang)
    cos_ref[...] = jnp.concatenate([c, c], axis=0)
    sin_ref[...] = jnp.concatenate([-s, s], axis=0)


def _rope_tables(positions):
    bsz, seq = positions.shape
    half = MLA_ROPE // 2
    inv_freq = (ROPE_THETA ** (-jnp.arange(half, dtype=F32) / half)).reshape(half, 1)
    ts = min(seq, 2048)
    spec = pl.BlockSpec((None, MLA_ROPE, ts), lambda b, s: (b, 0, s))
    return pl.pallas_call(
        _rope_kernel,
        grid=(bsz, seq // ts),
        in_specs=[pl.BlockSpec((None, 1, ts), lambda b, s: (b, 0, s)),
                  pl.BlockSpec((half, 1), lambda b, s: (0, 0))],
        out_specs=[spec, spec],
        out_shape=[jax.ShapeDtypeStruct((bsz, MLA_ROPE, seq), F32)] * 2,
        compiler_params=_cparams(("parallel", "parallel")),
        name="rope",
    )(positions.reshape(bsz, 1, seq), inv_freq)


def _attn_operand_specs(bsz, seq):
    nt = seq // TILE
    qt_spec = pl.BlockSpec((None, HEADS, None, QK_PAD, TILE), lambda b, s: (b, 0, s, 0, 0))
    k_spec = pl.BlockSpec((None, TILE, HEADS * QK_PAD), lambda b, s: (b, s, 0))
    vt_spec = pl.BlockSpec((None, HEADS, None, V_ROWS, TILE), lambda b, s: (b, 0, s, 0, 0))
    shapes = [jax.ShapeDtypeStruct((bsz, HEADS, nt, QK_PAD, TILE), BF16),
              jax.ShapeDtypeStruct((bsz, seq, HEADS * QK_PAD), BF16),
              jax.ShapeDtypeStruct((bsz, HEADS, nt, V_ROWS, TILE), BF16)]
    return [qt_spec, k_spec, vt_spec], shapes


def _store_vt(vt_ref, vt):
    row = lax.broadcasted_iota(jnp.int32, (V_ROWS - 64, TILE), 0)
    tail = jnp.where(row == 0, 1.0, 0.0).astype(BF16)
    for h in range(HEADS):
        vt_ref[h, 0:64, :] = vt[64 * h:64 * h + 64].astype(BF16)
        vt_ref[h, 64:V_ROWS, :] = tail


def _full(shape):
    return pl.BlockSpec(shape, lambda b, s: (0,) * len(shape))


def _mla_operands(hb, c, qg_ref, kvg_ref, wuq_ref, wuqsw_ref, wkpe_ref, wk_ref, wv_ref,
                  cos_ref, sin_ref, qt_ref, k_ref, vt_ref):
    scale = (MLA_NOPE + MLA_ROPE) ** -0.5 * LOG2E
    cqn = _rms_norm(c[:, :MLA_Q_LORA], qg_ref[...]).astype(BF16)
    kvn = _rms_norm(c[:, MLA_Q_LORA:MLA_Q_LORA + MLA_KV_LORA], kvg_ref[...]).astype(BF16)
    cos_t = cos_ref[...]
    sin_t = sin_ref[...]

    q_t = _nt(wuq_ref[...], cqn)
    qsw_t = _nt(wuqsw_ref[...], cqn)
    dq = MLA_NOPE + MLA_ROPE
    pad = jnp.zeros((QK_PAD - dq, TILE), BF16)
    for h in range(HEADS):
        nope = (q_t[dq * h:dq * h + MLA_NOPE] * scale).astype(BF16)
        rope = (q_t[dq * h + MLA_NOPE:dq * (h + 1)] * cos_t
                + qsw_t[MLA_ROPE * h:MLA_ROPE * (h + 1)] * sin_t)
        rope = (rope * scale).astype(BF16)
        if h % 2 == 0:
            qt_ref[h, 0:MLA_NOPE, :] = nope
            qt_ref[h, MLA_NOPE:dq, :] = rope
            qt_ref[h, dq:QK_PAD, :] = pad
        else:
            qt_ref[h, 0:MLA_ROPE, :] = rope
            qt_ref[h, MLA_ROPE:QK_PAD - MLA_NOPE, :] = pad
            qt_ref[h, QK_PAD - MLA_NOPE:QK_PAD, :] = nope

    kpe_t = _nt(wkpe_ref[...], hb)
    kr_t = kpe_t[0:MLA_ROPE] * cos_t + kpe_t[MLA_ROPE:2 * MLA_ROPE] * sin_t
    gap = jnp.zeros((MLA_NOPE - MLA_ROPE, TILE), F32)
    kr = jnp.concatenate([kr_t, gap, kr_t, gap], axis=0).T
    k_nope = _dot(kvn, wk_ref[...])
    lane = lax.broadcasted_iota(jnp.int32, (TILE, 128), 1)
    for g in range(HEADS // 2):
        pair = k_nope[:, 128 * g:128 * (g + 1)]
        k_ref[:, 256 * g:256 * g + 128] = jnp.where(lane < 64, pair, kr).astype(BF16)
        k_ref[:, 256 * g + 128:256 * (g + 1)] = jnp.where(lane < 64, kr, pair).astype(BF16)

    _store_vt(vt_ref, _nt(wv_ref[...], kvn))


MLA_WEIGHT_NAMES = ("qg", "kvg", "wuq", "wuqsw", "wkpe", "wk", "wv")
FOX_WEIGHT_NAMES = ("wk", "bfl", "wq", "wv", "tri", "pk", "pq")


def _fox_operands(hb, z, wk_ref, bfl_ref, wq_ref, wv_ref, tri_ref, pk_ref, pq_ref,
                  qt_ref, k_ref, vt_ref, carry_ref):
    @pl.when(pl.program_id(1) == 0)
    def _():
        carry_ref[...] = jnp.zeros_like(carry_ref)

    lane = lax.broadcasted_iota(jnp.int32, (TILE, 128), 1)
    z = z + bfl_ref[...]
    log_f = jnp.minimum(z, 0.0) - jnp.log1p(jnp.exp(-jnp.abs(z)))
    hi, mid, lo = _split3(log_f)
    parts = jnp.where(lane < 8, hi, jnp.where(lane < 16, mid, jnp.where(lane < 24, lo, 0.0)))
    cum = _dot(tri_ref[...], parts.astype(BF16))
    f = cum + pltpu.roll(cum, 120, 1) + pltpu.roll(cum, 112, 1) + carry_ref[...]
    f = jnp.where(lane < 8, f, 0.0)
    carry_ref[...] = f[TILE - 1:TILE, :]

    fhi, fmid, flo = _split3(f * LOG2E)
    p = (fhi + pltpu.roll(fmid, 8, 1) + pltpu.roll(flo, 16, 1)
         + jnp.where(lane == 24, 1.0, 0.0)).astype(BF16)
    fk = _dot(hb, wk_ref[...])
    aug = _dot(p, pk_ref[...])
    aug_t = _nt(pq_ref[...], p)
    q_t = _nt(wq_ref[...], hb)
    scale = FOX_DIM ** -0.5 * LOG2E
    for g in range(HEADS // 2):
        fk_pair = fk[:, 128 * g:128 * (g + 1)]
        aug_pair = aug[:, 128 * g:128 * (g + 1)]
        k_ref[:, 256 * g:256 * g + 128] = jnp.where(lane < 64, fk_pair, aug_pair).astype(BF16)
        k_ref[:, 256 * g + 128:256 * (g + 1)] = jnp.where(lane < 64, aug_pair, fk_pair).astype(BF16)
        he, ho = 2 * g, 2 * g + 1
        qt_ref[he, 0:64, :] = (q_t[64 * he:64 * he + 64] * scale).astype(BF16)
        qt_ref[he, 64:80, :] = aug_t[16 * he:16 * he + 16].astype(BF16)
        qt_ref[he, 80:QK_PAD, :] = jnp.zeros((QK_PAD - 80, TILE), BF16)
        qt_ref[ho, 0:16, :] = aug_t[16 * ho:16 * ho + 16].astype(BF16)
        qt_ref[ho, 16:64, :] = jnp.zeros((48, TILE), BF16)
        qt_ref[ho, 64:QK_PAD, :] = (q_t[64 * ho:64 * ho + 64] * scale).astype(BF16)

    _store_vt(vt_ref, _nt(wv_ref[...], hb))


MAX_LAG = 16.0
MAX_OVERSHOOT = 96.0
DIAG_TILES_PER_STEP = 4
FAST_CHUNKS_PER_STEP = 14
FAST_PIECE = 256
FAST_LOOKAHEAD = 3


def _attn_kernel(qt_ref, k_ref, vt_ref, o_ref, s0_ref, s1_ref, s2_ref, s3_ref, cm_ref, m_ref, acc_ref,
                 flag_ref, *, n_tiles):
    t = TILE
    nq = n_tiles // 2
    row = lax.broadcasted_iota(jnp.int32, (t, t), 0)
    col = lax.broadcasted_iota(jnp.int32, (t, t), 1)

    def logits(kblk, qtile, half, s_ref, slot):
        k = k_ref[pl.ds(pl.multiple_of(kblk * t, t), t), :]
        s = _dot(k, qt_ref[2 * qtile + half])
        s_ref[:, half * t:(half + 1) * t] = s
        cm_ref[slot, :, half * t:(half + 1) * t] = jnp.max(s, axis=0, keepdims=True)

    def softmax_pv(kblk, qtile, half, s_ref, slot, diagonal=False, first=False):
        cols = slice(half * t, (half + 1) * t)
        s = s_ref[:, cols]
        if diagonal:
            s = jnp.where(row > col, -jnp.inf, s)
            cm = jnp.max(s, axis=0, keepdims=True)
        else:
            cm = cm_ref[slot, :, cols]
        m_old = cm if first else m_ref[qtile, :, cols]
        m_new = cm if first else jnp.maximum(m_old, cm)
        p = jnp.exp2(s - m_new).astype(BF16)
        pv = _dot(vt_ref[kblk], p)
        if first:
            acc_ref[qtile, :, cols] = pv
        else:
            acc_ref[qtile, :, cols] = jnp.exp2(m_old - m_new) * acc_ref[qtile, :, cols] + pv
        m_ref[qtile, :, cols] = m_new

    def diag_logits(q, sa_ref, sb_ref, slot):
        logits(2 * q, q, 0, sa_ref, slot)
        logits(2 * q, q, 1, sa_ref, slot)
        logits(2 * q + 1, q, 1, sb_ref, slot + 1)

    def diag_softmax_pv(q, sa_ref, sb_ref, slot):
        softmax_pv(2 * q, q, 0, sa_ref, slot, diagonal=True, first=True)
        softmax_pv(2 * q, q, 1, sa_ref, slot, first=True)
        softmax_pv(2 * q + 1, q, 1, sb_ref, slot + 1, diagonal=True)

    def diagonal_phase():
        n = DIAG_TILES_PER_STEP if nq % DIAG_TILES_PER_STEP == 0 else 2
        bufs = ((s0_ref, s1_ref, 0), (s2_ref, s3_ref, 2))
        diag_logits(0, *bufs[0])

        def step(i, c):
            for r in range(n):
                cur = n * i + r
                diag_logits(jnp.minimum(cur + 1, nq - 1), *bufs[(r + 1) % 2])
                diag_softmax_pv(cur, *bufs[r % 2])
            return c

        lax.fori_loop(0, nq // n, step, 0)

    def advance(q, j):
        wrap = j + 1 >= q
        return jnp.where(wrap, jnp.minimum(q + 1, nq - 1), q), jnp.where(wrap, 0, j + 1)

    n_chunks = nq * (nq - 1) // 2
    first_chunk = (jnp.int32(1), jnp.int32(0))

    pieces_per_unit = 2 * t // FAST_PIECE

    def piece_logits(unit, piece):
        q, j, half = unit
        start = pl.multiple_of(j * 2 * t + piece * FAST_PIECE, FAST_PIECE)
        return _dot(k_ref[pl.ds(start, FAST_PIECE), :], qt_ref[2 * q + half])

    def fast_units(units):
        seq = [(u, piece) for u in units for piece in range(pieces_per_unit)]
        ahead = [piece_logits(*seq[i]) for i in range(min(FAST_LOOKAHEAD, len(seq)))]
        for i, (unit, piece) in enumerate(seq):
            q, j, half = unit
            cols = slice(half * t, (half + 1) * t)
            s = ahead.pop(0)
            if i + FAST_LOOKAHEAD < len(seq):
                ahead.append(piece_logits(*seq[i + FAST_LOOKAHEAD]))
            if piece == 0:
                m_old = m_ref[q, :, cols]
                cm, pv = None, None
            pm = jnp.max(s, axis=0, keepdims=True)
            cm = pm if cm is None else jnp.maximum(cm, pm)
            p = jnp.exp2(s - m_old).astype(BF16)
            blk, off = divmod(piece * FAST_PIECE, t)
            d = _dot(vt_ref[2 * j + blk, :, off:off + FAST_PIECE], p)
            pv = d if pv is None else pv + d
            if piece == pieces_per_unit - 1:
                m_new = jnp.maximum(m_old, cm - MAX_LAG)
                acc_ref[q, :, cols] = (acc_ref[q, :, cols] + pv) * jnp.exp2(m_old - m_new)
                m_ref[q, :, cols] = m_new
                flag_ref[:, cols] = jnp.maximum(flag_ref[:, cols], cm - m_old)

    def fast_phase():
        def chunks_from(c, n):
            units = []
            for _ in range(n):
                units += [(c[0], c[1], 0), (c[0], c[1], 1)]
                c = advance(*c)
            return units, c

        def step(i, c):
            units, c = chunks_from(c, FAST_CHUNKS_PER_STEP)
            fast_units(units)
            return c

        c = lax.fori_loop(0, n_chunks // FAST_CHUNKS_PER_STEP, step, first_chunk)
        if n_chunks % FAST_CHUNKS_PER_STEP:
            fast_units(chunks_from(c, n_chunks % FAST_CHUNKS_PER_STEP)[0])

    def safe_chunk(q, j, qn, jn):
        for half in range(2):
            logits(2 * j + 1, q, half, s1_ref, 1)
        for half in range(2):
            softmax_pv(2 * j, q, half, s0_ref, 0)
        for half in range(2):
            logits(2 * jn, qn, half, s0_ref, 0)
        for half in range(2):
            softmax_pv(2 * j + 1, q, half, s1_ref, 1)

    def safe_phase():
        for half in range(2):
            logits(0, 1, half, s0_ref, 0)

        def step(i, c):
            nxt = advance(*c)
            safe_chunk(*c, *nxt)
            return nxt

        lax.fori_loop(0, n_chunks, step, first_chunk)

    def finalize():
        def body(q, c):
            for half in range(2):
                cols = slice(half * t, (half + 1) * t)
                o_ref[2 * q + half] = (acc_ref[q, 0:64, cols] / acc_ref[q, 64:65, cols]).astype(BF16)
            return c

        lax.fori_loop(0, nq, body, 0)

    flag_ref[...] = jnp.full(flag_ref.shape, -jnp.inf, F32)
    diagonal_phase()
    if n_chunks:
        fast_phase()
    finalize()

    @pl.when(jnp.max(flag_ref[...]) > MAX_OVERSHOOT)
    def _():
        diagonal_phase()
        if n_chunks:
            safe_phase()
        finalize()


def _attention(qt, k, vt):
    bsz, heads, nt = qt.shape[:3]
    seq = k.shape[1]
    return pl.pallas_call(
        functools.partial(_attn_kernel, n_tiles=nt),
        grid=(bsz, heads),
        in_specs=[pl.BlockSpec((None, None, nt, QK_PAD, TILE), lambda b, h: (b, h, 0, 0, 0)),
                  pl.BlockSpec((None, seq, QK_PAD), lambda b, h: (b, 0, h)),
                  pl.BlockSpec((None, None, nt, V_ROWS, TILE), lambda b, h: (b, h, 0, 0, 0))],
        out_specs=pl.BlockSpec((None, None, nt, 64, TILE), lambda b, h: (b, h, 0, 0, 0)),
        out_shape=jax.ShapeDtypeStruct((bsz, heads, nt, 64, TILE), BF16),
        scratch_shapes=[pltpu.VMEM((TILE, 2 * TILE), F32)] * 4
        + [pltpu.VMEM((4, 1, 2 * TILE), F32), pltpu.VMEM((nt // 2, 1, 2 * TILE), F32),
           pltpu.VMEM((nt // 2, V_ROWS, 2 * TILE), F32), pltpu.VMEM((1, 2 * TILE), F32)],
        compiler_params=_cparams(("parallel", "parallel")),
        name="attn",
    )(qt, k, vt)


def _mem_kv_kernel(mem_ref, w_ref, mk_ref, mv_ref):
    kv = _dot(mem_ref[...].astype(BF16), w_ref[...])
    mk_ref[...] = kv[:, :MEM_W].astype(BF16)
    mv_ref[...] = kv[:, MEM_W:].astype(BF16)


def _mem_kv(mem, w):
    bsz = mem.shape[0]
    spec = pl.BlockSpec((None, MEM_LEN, MEM_W), lambda b: (b, 0, 0))
    return pl.pallas_call(
        _mem_kv_kernel,
        grid=(bsz,),
        in_specs=[pl.BlockSpec((None, MEM_LEN, D_MODEL), lambda b: (b, 0, 0)),
                  pl.BlockSpec(w.shape, lambda b: (0, 0))],
        out_specs=[spec, spec],
        out_shape=[jax.ShapeDtypeStruct((bsz, MEM_LEN, MEM_W), BF16)] * 2,
        compiler_params=_cparams(("parallel",)),
        name="mem_kv",
    )(mem, w)


def _mem_branch(hb, wq_ref, mk_ref, mv_ref, o_ref):
    scale = MEM_DIM ** -0.5
    mq = _dot(hb, wq_ref[...]).astype(BF16)
    for h in range(MEM_HEADS):
        sl = slice(MEM_DIM * h, MEM_DIM * (h + 1))
        s = _nt(mq[:, sl], mk_ref[:, sl]) * scale
        e = jnp.exp(s - jnp.max(s, axis=-1, keepdims=True))
        o = _dot(e.astype(BF16), mv_ref[:, sl]) / jnp.sum(e, axis=-1, keepdims=True)
        o_ref[:, sl] = o.astype(BF16)


def _mixer_prep_kernel(*refs):
    n_mla, n_fox = len(MLA_WEIGHT_NAMES), len(FOX_WEIGHT_NAMES)
    refs = list(refs)
    h_ref, wc_ref = refs.pop(0), refs.pop(0)
    mla_in = [refs.pop(0) for _ in range(n_mla + 2)]
    fox_in = [refs.pop(0) for _ in range(n_fox)]
    mem_in = [refs.pop(0) for _ in range(3)]
    mla_out = [refs.pop(0) for _ in range(3)]
    fox_out = [refs.pop(0) for _ in range(3)]
    omem_ref, carry_ref = refs
    hb = h_ref[...].astype(BF16)
    c = _dot(hb, wc_ref[...])
    _mla_operands(hb, c, *mla_in, *mla_out)
    _fox_operands(hb, c[:, MLA_Q_LORA + MLA_KV_LORA:], *fox_in, *fox_out, carry_ref)
    _mem_branch(hb, *mem_in, omem_ref)


def _mixer_prep(h, wc, mla_w, cos_t, sin_t, fox_w, wq, mk, mv):
    bsz, seq, _ = h.shape
    op_specs, op_shapes = _attn_operand_specs(bsz, seq)
    tab_spec = pl.BlockSpec((None, MLA_ROPE, TILE), lambda b, s: (b, 0, s))
    kv_spec = pl.BlockSpec((None, MEM_LEN, MEM_W), lambda b, s: (b, 0, 0))
    mla_args = [mla_w[n] for n in MLA_WEIGHT_NAMES]
    fox_args = [fox_w[n] for n in FOX_WEIGHT_NAMES]
    outs = pl.pallas_call(
        _mixer_prep_kernel,
        grid=(bsz, seq // TILE),
        in_specs=[pl.BlockSpec((None, TILE, D_MODEL), lambda b, s: (b, s, 0)), _full(wc.shape)]
        + [_full(a.shape) for a in mla_args] + [tab_spec, tab_spec]
        + [_full(a.shape) for a in fox_args] + [_full(wq.shape), kv_spec, kv_spec],
        out_specs=op_specs + op_specs + [pl.BlockSpec((None, TILE, MEM_W), lambda b, s: (b, s, 0))],
        out_shape=op_shapes + op_shapes + [jax.ShapeDtypeStruct((bsz, seq, MEM_W), BF16)],
        scratch_shapes=[pltpu.VMEM((1, 128), F32)],
        compiler_params=_cparams(("parallel", "arbitrary")),
        name="mixer_prep",
    )(h, wc, *mla_args, cos_t, sin_t, *fox_args, wq, mk, mv)
    return outs[0:3], outs[3:6], outs[6]


def _merge_kernel(h_ref, omla_ref, ofox_ref, omem_ref, wg_ref, bg_ref, wbm_ref, wbf_ref, wbc_ref,
                  wo_ref, g_ref, b_ref, o_ref):
    rows = MERGE_TILE // MERGE_ROW_GROUPS
    for r in range(MERGE_ROW_GROUPS):
        sl = slice(r * rows, (r + 1) * rows)
        tile, off = divmod(r * rows, TILE)
        lanes = slice(off, off + rows)
        h = h_ref[sl, :]
        gates = jax.nn.sigmoid(_dot(h.astype(BF16), wg_ref[...]) + bg_ref[...])
        a_mla = _tn(omla_ref[:, tile].reshape(MLA_W, TILE)[:, lanes], wbm_ref[...])
        a_fox = _tn(ofox_ref[:, tile].reshape(FOX_W, TILE)[:, lanes], wbf_ref[...])
        a_mem = _dot(omem_ref[sl, :], wbc_ref[...])
        merged = (gates[:, :D_MODEL] * a_mla + gates[:, D_MODEL:2 * D_MODEL] * a_fox
                  + gates[:, 2 * D_MODEL:] * a_mem)
        mix = _dot(merged.astype(BF16), wo_ref[...])
        o_ref[sl, :] = _layer_norm(ALPHA * h + mix, g_ref[...], b_ref[...])


def _merge(h, o_mla, o_fox, o_mem, w):
    bsz, seq, _ = h.shape
    ot_spec = pl.BlockSpec((None, HEADS, MERGE_TILE // TILE, 64, TILE), lambda b, s: (b, 0, s, 0, 0))
    names = ["wg", "bg", "wbm", "wbf", "wbc", "wo", "g", "b"]
    return pl.pallas_call(
        _merge_kernel,
        grid=(bsz, seq // MERGE_TILE),
        in_specs=[pl.BlockSpec((None, MERGE_TILE, D_MODEL), lambda b, s: (b, s, 0)), ot_spec, ot_spec,
                  pl.BlockSpec((None, MERGE_TILE, MEM_W), lambda b, s: (b, s, 0))]
        + [_full(w[n].shape) for n in names],
        out_specs=pl.BlockSpec((None, MERGE_TILE, D_MODEL), lambda b, s: (b, s, 0)),
        out_shape=jax.ShapeDtypeStruct((bsz, seq, D_MODEL), F32),
        compiler_params=_cparams(("parallel", "parallel")),
        name="merge",
    )(h, o_mla, o_fox, o_mem, *[w[n] for n in names])


def _placement_matrices():
    pk = np.zeros((128, HEADS * 64), np.float32)
    pq = np.zeros((HEADS * 16, 128), np.float32)
    for h in range(HEADS):
        base = 128 * (h // 2) + (64 if h % 2 == 0 else 0)
        for part in range(3):
            pk[8 * part + h, base + part] = -1.0
            pk[24, base + 3 + part] = 1.0
            pq[16 * h + part, 24] = 1.0
            pq[16 * h + 3 + part, 8 * part + h] = 1.0
    tri = np.tril(np.ones((TILE, TILE), np.float32))
    return jnp.asarray(pk, BF16), jnp.asarray(pq, BF16), jnp.asarray(tri, BF16)


def _swap_halves(w, axis):
    a, b = jnp.split(w, 2, axis=axis)
    return jnp.concatenate([b, a], axis=axis)


def kernel(x, mem, positions, ln1_g, ln1_b, ffn1_w_in, ffn1_w_down, w_in, b_gate, mla_q_norm, mla_w_uq, mla_kv_norm, mla_w_ukv, fox_b_f, mem_w_kv, w_br_mla, w_br_fox, w_br_mem, w_out, ln2_g, ln2_b, ffn2_w_in, ffn2_w_down, ln3_g, ln3_b):
    bsz, seq, _ = x.shape
    n = bsz * seq
    row = lambda v: v.reshape(1, -1).astype(F32)
    bf = lambda v: v.astype(BF16)

    sizes = (MLA_Q_LORA, MLA_KV_LORA, MLA_ROPE, FOX_W, FOX_W, FOX_W, FOX_HEADS, MEM_W)
    offs = np.cumsum((0,) + sizes).tolist()
    wi = w_in[0]
    w_cq, w_ckv, w_kpe, w_fq, w_fk, w_fv, w_fl, w_mq = (wi[:, offs[i]:offs[i + 1]] for i in range(8))
    w_gates = wi[:, offs[8]:]

    uq = mla_w_uq[0].reshape(MLA_Q_LORA, MLA_HEADS, MLA_NOPE + MLA_ROPE)
    uq_rope_sw = _swap_halves(uq[:, :, MLA_NOPE:], axis=2).reshape(MLA_Q_LORA, MLA_HEADS * MLA_ROPE)
    ukv = mla_w_ukv[0].reshape(MLA_KV_LORA, MLA_HEADS, MLA_NOPE + MLA_V)
    w_kpe_t = jnp.concatenate([w_kpe.T, _swap_halves(w_kpe, axis=1).T,
                               jnp.zeros((QK_PAD - 2 * MLA_ROPE, D_MODEL), F32)], axis=0)
    mla_w = {
        "qg": row(mla_q_norm[0]), "kvg": row(mla_kv_norm[0]),
        "wuq": bf(mla_w_uq[0].T), "wuqsw": bf(uq_rope_sw.T), "wkpe": bf(w_kpe_t),
        "wk": bf(ukv[:, :, :MLA_NOPE].reshape(MLA_KV_LORA, -1)),
        "wv": bf(ukv[:, :, MLA_NOPE:].reshape(MLA_KV_LORA, -1).T),
    }

    pk, pq, tri = _placement_matrices()
    w_fl3 = jnp.concatenate([w_fl, w_fl, w_fl, jnp.zeros((D_MODEL, 128 - 3 * FOX_HEADS), F32)], axis=1)
    b_fl3 = jnp.concatenate([fox_b_f[0]] * 3 + [jnp.zeros((128 - 3 * FOX_HEADS,), F32)]).reshape(1, 128)
    w_c = bf(jnp.concatenate([w_cq, w_ckv, w_fl3], axis=1))
    fox_w = {
        "wk": bf(w_fk), "bfl": b_fl3,
        "wq": bf(w_fq.T), "wv": bf(w_fv.T), "tri": tri, "pk": pk, "pq": pq,
    }

    merge_w = {
        "wg": bf(w_gates), "bg": row(b_gate[0]), "wbm": bf(w_br_mla[0]), "wbf": bf(w_br_fox[0]),
        "wbc": bf(w_br_mem[0]), "wo": bf(w_out[0]), "g": row(ln2_g[0]), "b": row(ln2_b[0]),
    }

    h1 = _ffn_ln(x.reshape(n, D_MODEL), bf(ffn1_w_in[0]), bf(ffn1_w_down[0]), row(ln1_g[0]), row(ln1_b[0]))
    h1 = h1.reshape(bsz, seq, D_MODEL)

    cos_t, sin_t = _rope_tables(positions)
    mk, mv = _mem_kv(mem, bf(mem_w_kv[0]))
    mla_ops, fox_ops, o_mem = _mixer_prep(h1, w_c, mla_w, cos_t, sin_t, fox_w, bf(w_mq), mk, mv)
    o_mla = _attention(*mla_ops)
    o_fox = _attention(*fox_ops)

    h2 = _merge(h1, o_mla, o_fox, o_mem, merge_w)
    out = _ffn_ln(h2.reshape(n, D_MODEL), bf(ffn2_w_in[0]), bf(ffn2_w_down[0]), row(ln3_g[0]), row(ln3_b[0]))
    return out.reshape(bsz, seq, D_MODEL)
```

```python
import functools

import numpy as np
import jax
import jax.numpy as jnp
from jax import lax
from jax.experimental import pallas as pl
from jax.experimental.pallas import tpu as pltpu

F32 = jnp.float32
BF16 = jnp.bfloat16

D_MODEL = 1024
MEM_LEN = 256
MLA_HEADS = 8
MLA_NOPE = 64
MLA_ROPE = 32
MLA_V = 64
MLA_Q_LORA = 384
MLA_KV_LORA = 256
FOX_HEADS = 8
FOX_DIM = 64
MEM_HEADS = 4
MEM_DIM = 128
D_FF = 2816
ROPE_THETA = 10000.0
LN_EPS = 1e-5
RMS_EPS = 1e-6
DEPTH = 1
ALPHA = (2.0 * DEPTH) ** 0.25
MLA_W = MLA_HEADS * MLA_V
FOX_W = FOX_HEADS * FOX_DIM
MEM_W = MEM_HEADS * MEM_DIM

HEADS = 8
QK_PAD = 128
V_ROWS = 80
TILE = 512
FF_TILE = 1024
FF_ROW_GROUPS = 4
MERGE_TILE = 1024
MERGE_ROW_GROUPS = 4
VMEM_LIMIT = 56 * 1024 * 1024
LOG2E = 1.4426950408889634


def _cparams(sem):
    return pltpu.CompilerParams(dimension_semantics=sem, vmem_limit_bytes=VMEM_LIMIT)


def _nt(a, b):
    return lax.dot_general(a, b, (((1,), (1,)), ((), ())), preferred_element_type=F32)


def _tn(a, b):
    return lax.dot_general(a, b, (((0,), (0,)), ((), ())), preferred_element_type=F32)


def _dot(a, b):
    return jnp.dot(a, b, preferred_element_type=F32)


def _layer_norm(y, g, b):
    mu = jnp.mean(y, axis=-1, keepdims=True)
    d = y - mu
    var = jnp.mean(d * d, axis=-1, keepdims=True)
    return d * lax.rsqrt(var + LN_EPS) * g + b


def _rms_norm(x, g):
    return x * lax.rsqrt(jnp.mean(x * x, axis=-1, keepdims=True) + RMS_EPS) * g


def _split3(x):
    hi = x.astype(BF16).astype(F32)
    r = x - hi
    mid = r.astype(BF16).astype(F32)
    lo = (r - mid).astype(BF16).astype(F32)
    return hi, mid, lo


def _ffn_ln_kernel(x_ref, wa_ref, wb_ref, wd_ref, g_ref, b_ref, o_ref):
    rows = FF_TILE // FF_ROW_GROUPS
    for r in range(FF_ROW_GROUPS):
        sl = slice(r * rows, (r + 1) * rows)
        x = x_ref[sl, :]
        xb = x.astype(BF16)
        a = _dot(xb, wa_ref[...])
        b = _dot(xb, wb_ref[...])
        gated = (a * jax.nn.sigmoid(a) * b).astype(BF16)
        y = ALPHA * x + 0.5 * _dot(gated, wd_ref[...])
        o_ref[sl, :] = _layer_norm(y, g_ref[...], b_ref[...])


def _resident(shape, index_map):
    return pl.BlockSpec(shape, index_map, pipeline_mode=pl.Buffered(1))


def _ffn_ln(x, w_in, w_down, g, b):
    n = x.shape[0]
    return pl.pallas_call(
        _ffn_ln_kernel,
        grid=(n // FF_TILE,),
        in_specs=[
            pl.BlockSpec((FF_TILE, D_MODEL), lambda i: (i, 0)),
            _resident((D_MODEL, D_FF), lambda i: (0, 0)),
            _resident((D_MODEL, D_FF), lambda i: (0, 1)),
            _resident((D_FF, D_MODEL), lambda i: (0, 0)),
            _resident((1, D_MODEL), lambda i: (0, 0)),
            _resident((1, D_MODEL), lambda i: (0, 0)),
        ],
        out_specs=pl.BlockSpec((FF_TILE, D_MODEL), lambda i: (i, 0)),
        out_shape=jax.ShapeDtypeStruct((n, D_MODEL), F32),
        compiler_params=_cparams(("parallel",)),
        name="ffn_ln",
    )(x, w_in, w_in, w_down, g, b)


def _rope_kernel(pos_ref, inv_ref, cos_ref, sin_ref):
    ang = pos_ref[...].astype(F32) * inv_ref[...]
    c = jnp.cos(ang)
    s = jnp.sin(ang)
    cos_ref[...] = jnp.concatenate([c, c], axis=0)
    sin_ref[...] = jnp.concatenate([-s, s], axis=0)


def _rope_tables(positions):
    bsz, seq = positions.shape
    half = MLA_ROPE // 2
    inv_freq = (ROPE_THETA ** (-jnp.arange(half, dtype=F32) / half)).reshape(half, 1)
    ts = min(seq, 2048)
    spec = pl.BlockSpec((None, MLA_ROPE, ts), lambda b, s: (b, 0, s))
    return pl.pallas_call(
        _rope_kernel,
        grid=(bsz, seq // ts),
        in_specs=[pl.BlockSpec((None, 1, ts), lambda b, s: (b, 0, s)),
                  pl.BlockSpec((half, 1), lambda b, s: (0, 0))],
        out_specs=[spec, spec],
        out_shape=[jax.ShapeDtypeStruct((bsz, MLA_ROPE, seq), F32)] * 2,
        compiler_params=_cparams(("parallel", "parallel")),
        name="rope",
    )(positions.reshape(bsz, 1, seq), inv_freq)


def _attn_operand_specs(bsz, seq):
    nt = seq // TILE
    qt_spec = pl.BlockSpec((None, HEADS, None, QK_PAD, TILE), lambda b, s: (b, 0, s, 0, 0))
    k_spec = pl.BlockSpec((None, TILE, HEADS * QK_PAD), lambda b, s: (b, s, 0))
    vt_spec = pl.BlockSpec((None, HEADS, None, V_ROWS, TILE), lambda b, s: (b, 0, s, 0, 0))
    shapes = [jax.ShapeDtypeStruct((bsz, HEADS, nt, QK_PAD, TILE), BF16),
              jax.ShapeDtypeStruct((bsz, seq, HEADS * QK_PAD), BF16),
              jax.ShapeDtypeStruct((bsz, HEADS, nt, V_ROWS, TILE), BF16)]
    return [qt_spec, k_spec, vt_spec], shapes


def _store_vt(vt_ref, vt):
    row = lax.broadcasted_iota(jnp.int32, (V_ROWS - 64, TILE), 0)
    tail = jnp.where(row == 0, 1.0, 0.0).astype(BF16)
    for h in range(HEADS):
        vt_ref[h, 0:64, :] = vt[64 * h:64 * h + 64].astype(BF16)
        vt_ref[h, 64:V_ROWS, :] = tail


def _full(shape):
    return pl.BlockSpec(shape, lambda b, s: (0,) * len(shape))


def _mla_operands(hb, c, qg_ref, kvg_ref, wuq_ref, wuqsw_ref, wkpe_ref, wk_ref, wv_ref,
                  cos_ref, sin_ref, qt_ref, k_ref, vt_ref):
    scale = (MLA_NOPE + MLA_ROPE) ** -0.5 * LOG2E
    cqn = _rms_norm(c[:, :MLA_Q_LORA], qg_ref[...]).astype(BF16)
    kvn = _rms_norm(c[:, MLA_Q_LORA:MLA_Q_LORA + MLA_KV_LORA], kvg_ref[...]).astype(BF16)
    cos_t = cos_ref[...]
    sin_t = sin_ref[...]

    q_t = _nt(wuq_ref[...], cqn)
    qsw_t = _nt(wuqsw_ref[...], cqn)
    dq = MLA_NOPE + MLA_ROPE
    pad = jnp.zeros((QK_PAD - dq, TILE), BF16)
    for h in range(HEADS):
        nope = (q_t[dq * h:dq * h + MLA_NOPE] * scale).astype(BF16)
        rope = (q_t[dq * h + MLA_NOPE:dq * (h + 1)] * cos_t
                + qsw_t[MLA_ROPE * h:MLA_ROPE * (h + 1)] * sin_t)
        rope = (rope * scale).astype(BF16)
        if h % 2 == 0:
            qt_ref[h, 0:MLA_NOPE, :] = nope
            qt_ref[h, MLA_NOPE:dq, :] = rope
            qt_ref[h, dq:QK_PAD, :] = pad
        else:
            qt_ref[h, 0:MLA_ROPE, :] = rope
            qt_ref[h, MLA_ROPE:QK_PAD - MLA_NOPE, :] = pad
            qt_ref[h, QK_PAD - MLA_NOPE:QK_PAD, :] = nope

    kpe_t = _nt(wkpe_ref[...], hb)
    kr_t = kpe_t[0:MLA_ROPE] * cos_t + kpe_t[MLA_ROPE:2 * MLA_ROPE] * sin_t
    gap = jnp.zeros((MLA_NOPE - MLA_ROPE, TILE), F32)
    kr = jnp.concatenate([kr_t, gap, kr_t, gap], axis=0).T
    k_nope = _dot(kvn, wk_ref[...])
    lane = lax.broadcasted_iota(jnp.int32, (TILE, 128), 1)
    for g in range(HEADS // 2):
        pair = k_nope[:, 128 * g:128 * (g + 1)]
        k_ref[:, 256 * g:256 * g + 128] = jnp.where(lane < 64, pair, kr).astype(BF16)
        k_ref[:, 256 * g + 128:256 * (g + 1)] = jnp.where(lane < 64, kr, pair).astype(BF16)

    _store_vt(vt_ref, _nt(wv_ref[...], kvn))


MLA_WEIGHT_NAMES = ("qg", "kvg", "wuq", "wuqsw", "wkpe", "wk", "wv")
FOX_WEIGHT_NAMES = ("wk", "bfl", "wq", "wv", "tri", "pk", "pq")


def _fox_operands(hb, z, wk_ref, bfl_ref, wq_ref, wv_ref, tri_ref, pk_ref, pq_ref,
                  qt_ref, k_ref, vt_ref, carry_ref):
    @pl.when(pl.program_id(1) == 0)
    def _():
        carry_ref[...] = jnp.zeros_like(carry_ref)

    lane = lax.broadcasted_iota(jnp.int32, (TILE, 128), 1)
    z = z + bfl_ref[...]
    log_f = jnp.minimum(z, 0.0) - jnp.log1p(jnp.exp(-jnp.abs(z)))
    hi, mid, lo = _split3(log_f)
    parts = jnp.where(lane < 8, hi, jnp.where(lane < 16, mid, jnp.where(lane < 24, lo, 0.0)))
    cum = _dot(tri_ref[...], parts.astype(BF16))
    f = cum + pltpu.roll(cum, 120, 1) + pltpu.roll(cum, 112, 1) + carry_ref[...]
    f = jnp.where(lane < 8, f, 0.0)
    carry_ref[...] = f[TILE - 1:TILE, :]

    fhi, fmid, flo = _split3(f * LOG2E)
    p = (fhi + pltpu.roll(fmid, 8, 1) + pltpu.roll(flo, 16, 1)
         + jnp.where(lane == 24, 1.0, 0.0)).astype(BF16)
    fk = _dot(hb, wk_ref[...])
    aug = _dot(p, pk_ref[...])
    aug_t = _nt(pq_ref[...], p)
    q_t = _nt(wq_ref[...], hb)
    scale = FOX_DIM ** -0.5 * LOG2E
    for g in range(HEADS // 2):
        fk_pair = fk[:, 128 * g:128 * (g + 1)]
        aug_pair = aug[:, 128 * g:128 * (g + 1)]
        k_ref[:, 256 * g:256 * g + 128] = jnp.where(lane < 64, fk_pair, aug_pair).astype(BF16)
        k_ref[:, 256 * g + 128:256 * (g + 1)] = jnp.where(lane < 64, aug_pair, fk_pair).astype(BF16)
        he, ho = 2 * g, 2 * g + 1
        qt_ref[he, 0:64, :] = (q_t[64 * he:64 * he + 64] * scale).astype(BF16)
        qt_ref[he, 64:80, :] = aug_t[16 * he:16 * he + 16].astype(BF16)
        qt_ref[he, 80:QK_PAD, :] = jnp.zeros((QK_PAD - 80, TILE), BF16)
        qt_ref[ho, 0:16, :] = aug_t[16 * ho:16 * ho + 16].astype(BF16)
        qt_ref[ho, 16:64, :] = jnp.zeros((48, TILE), BF16)
        qt_ref[ho, 64:QK_PAD, :] = (q_t[64 * ho:64 * ho + 64] * scale).astype(BF16)

    _store_vt(vt_ref, _nt(wv_ref[...], hb))


MAX_LAG = 16.0
MAX_OVERSHOOT = 96.0
DIAG_TILES_PER_STEP = 4
FAST_CHUNKS_PER_STEP = 14
FAST_PIECE = 256
FAST_LOOKAHEAD = 3


def _attn_kernel(qt_ref, k_ref, vt_ref, o_ref, s0_ref, s1_ref, s2_ref, s3_ref, cm_ref, m_ref, acc_ref,
                 flag_ref, *, n_tiles):
    t = TILE
    nq = n_tiles // 2
    row = lax.broadcasted_iota(jnp.int32, (t, t), 0)
    col = lax.broadcasted_iota(jnp.int32, (t, t), 1)

    def logits(kblk, qtile, half, s_ref, slot):
        k = k_ref[pl.ds(pl.multiple_of(kblk * t, t), t), :]
        s = _dot(k, qt_ref[2 * qtile + half])
        s_ref[:, half * t:(half + 1) * t] = s
        cm_ref[slot, :, half * t:(half + 1) * t] = jnp.max(s, axis=0, keepdims=True)

    def softmax_pv(kblk, qtile, half, s_ref, slot, diagonal=False, first=False, between=None):
        cols = slice(half * t, (half + 1) * t)
        s = s_ref[:, cols]
        if diagonal:
            s = jnp.where(row > col, -jnp.inf, s)
            cm = jnp.max(s, axis=0, keepdims=True)
        else:
            cm = cm_ref[slot, :, cols]
        m_old = cm if first else m_ref[qtile, :, cols]
        m_new = cm if first else jnp.maximum(m_old, cm)
        m_ref[qtile, :, cols] = m_new
        p = jnp.exp2(s - m_new).astype(BF16)
        if between is None:
            pv = _dot(vt_ref[kblk], p)
        else:
            pv = None
            for piece in range(2):
                keys = slice(piece * (t // 2), (piece + 1) * (t // 2))
                d = _dot(vt_ref[kblk, :, keys], p[keys])
                pv = d if pv is None else pv + d
                between()
        if first:
            acc_ref[qtile, :, cols] = pv
        else:
            acc_ref[qtile, :, cols] = jnp.exp2(m_old - m_new) * acc_ref[qtile, :, cols] + pv

    def diag_logits_pieces(q, sa_ref, sb_ref, slot):
        units = ((2 * q, 0, sa_ref, slot), (2 * q, 1, sa_ref, slot), (2 * q + 1, 1, sb_ref, slot + 1))

        def piece(kblk, half, s_ref, slot, part):
            cols = slice(half * t, (half + 1) * t)
            keys = slice(part * (t // 2), (part + 1) * (t // 2))
            start = pl.multiple_of(kblk * t + part * (t // 2), t // 2)
            s = _dot(k_ref[pl.ds(start, t // 2), :], qt_ref[2 * q + half])
            s_ref[keys, cols] = s
            pm = jnp.max(s, axis=0, keepdims=True)
            cm_ref[slot, :, cols] = pm if part == 0 else jnp.maximum(cm_ref[slot, :, cols], pm)

        return [functools.partial(piece, *u, part) for u in units for part in range(2)]

    def diagonal_phase():
        n = DIAG_TILES_PER_STEP if nq % DIAG_TILES_PER_STEP == 0 else 2
        bufs = ((s0_ref, s1_ref, 0), (s2_ref, s3_ref, 2))
        for thunk in diag_logits_pieces(0, *bufs[0]):
            thunk()

        def step(i, c):
            for r in range(n):
                cur = n * i + r
                pending = diag_logits_pieces(jnp.minimum(cur + 1, nq - 1), *bufs[(r + 1) % 2])

                def queue_next():
                    if pending:
                        pending.pop(0)()

                sa_ref, sb_ref, slot = bufs[r % 2]
                queue_next()
                queue_next()
                softmax_pv(2 * cur, cur, 0, sa_ref, slot, diagonal=True, first=True, between=queue_next)
                softmax_pv(2 * cur, cur, 1, sa_ref, slot, first=True, between=queue_next)
                softmax_pv(2 * cur + 1, cur, 1, sb_ref, slot + 1, diagonal=True, between=queue_next)
                while pending:
                    queue_next()
            return c

        lax.fori_loop(0, nq // n, step, 0)

    def advance(q, j):
        wrap = j + 1 >= q
        return jnp.where(wrap, jnp.minimum(q + 1, nq - 1), q), jnp.where(wrap, 0, j + 1)

    n_chunks = nq * (nq - 1) // 2
    first_chunk = (jnp.int32(1), jnp.int32(0))

    pieces_per_unit = 2 * t // FAST_PIECE

    def piece_logits(unit, piece):
        q, j, half = unit
        start = pl.multiple_of(j * 2 * t + piece * FAST_PIECE, FAST_PIECE)
        return _dot(k_ref[pl.ds(start, FAST_PIECE), :], qt_ref[2 * q + half])

    def fast_units(units):
        seq = [(u, piece) for u in units for piece in range(pieces_per_unit)]
        ahead = [piece_logits(*seq[i]) for i in range(min(FAST_LOOKAHEAD, len(seq)))]
        for i, (unit, piece) in enumerate(seq):
            q, j, half = unit
            cols = slice(half * t, (half + 1) * t)
            s = ahead.pop(0)
            if i + FAST_LOOKAHEAD < len(seq):
                ahead.append(piece_logits(*seq[i + FAST_LOOKAHEAD]))
            if piece == 0:
                m_old = m_ref[q, :, cols]
                cm, pv = None, None
            pm = jnp.max(s, axis=0, keepdims=True)
            cm = pm if cm is None else jnp.maximum(cm, pm)
            p = jnp.exp2(s - m_old).astype(BF16)
            blk, off = divmod(piece * FAST_PIECE, t)
            d = _dot(vt_ref[2 * j + blk, :, off:off + FAST_PIECE], p)
            pv = d if pv is None else pv + d
            if piece == pieces_per_unit - 1:
                m_new = jnp.maximum(m_old, cm - MAX_LAG)
                acc_ref[q, :, cols] = (acc_ref[q, :, cols] + pv) * jnp.exp2(m_old - m_new)
                m_ref[q, :, cols] = m_new
                flag_ref[:, cols] = jnp.maximum(flag_ref[:, cols], cm - m_old)

    def fast_phase():
        def chunks_from(c, n):
            units = []
            for _ in range(n):
                units += [(c[0], c[1], 0), (c[0], c[1], 1)]
                c = advance(*c)
            return units, c

        def step(i, c):
            units, c = chunks_from(c, FAST_CHUNKS_PER_STEP)
            fast_units(units)
            return c

        c = lax.fori_loop(0, n_chunks // FAST_CHUNKS_PER_STEP, step, first_chunk)
        if n_chunks % FAST_CHUNKS_PER_STEP:
            fast_units(chunks_from(c, n_chunks % FAST_CHUNKS_PER_STEP)[0])

    def safe_chunk(q, j, qn, jn):
        for half in range(2):
            logits(2 * j + 1, q, half, s1_ref, 1)
        for half in range(2):
            softmax_pv(2 * j, q, half, s0_ref, 0)
        for half in range(2):
            logits(2 * jn, qn, half, s0_ref, 0)
        for half in range(2):
            softmax_pv(2 * j + 1, q, half, s1_ref, 1)

    def safe_phase():
        for half in range(2):
            logits(0, 1, half, s0_ref, 0)

        def step(i, c):
            nxt = advance(*c)
            safe_chunk(*c, *nxt)
            return nxt

        lax.fori_loop(0, n_chunks, step, first_chunk)

    def finalize():
        def body(q, c):
            for half in range(2):
                cols = slice(half * t, (half + 1) * t)
                o_ref[2 * q + half] = (acc_ref[q, 0:64, cols] / acc_ref[q, 64:65, cols]).astype(BF16)
            return c

        lax.fori_loop(0, nq, body, 0)

    flag_ref[...] = jnp.full(flag_ref.shape, -jnp.inf, F32)
    diagonal_phase()
    if n_chunks:
        fast_phase()
    finalize()

    @pl.when(jnp.max(flag_ref[...]) > MAX_OVERSHOOT)
    def _():
        diagonal_phase()
        if n_chunks:
            safe_phase()
        finalize()


def _attention(qt, k, vt):
    bsz, heads, nt = qt.shape[:3]
    seq = k.shape[1]
    return pl.pallas_call(
        functools.partial(_attn_kernel, n_tiles=nt),
        grid=(bsz, heads),
        in_specs=[pl.BlockSpec((None, None, nt, QK_PAD, TILE), lambda b, h: (b, h, 0, 0, 0)),
                  pl.BlockSpec((None, seq, QK_PAD), lambda b, h: (b, 0, h)),
                  pl.BlockSpec((None, None, nt, V_ROWS, TILE), lambda b, h: (b, h, 0, 0, 0))],
        out_specs=pl.BlockSpec((None, None, nt, 64, TILE), lambda b, h: (b, h, 0, 0, 0)),
        out_shape=jax.ShapeDtypeStruct((bsz, heads, nt, 64, TILE), BF16),
        scratch_shapes=[pltpu.VMEM((TILE, 2 * TILE), F32)] * 4
        + [pltpu.VMEM((4, 1, 2 * TILE), F32), pltpu.VMEM((nt // 2, 1, 2 * TILE), F32),
           pltpu.VMEM((nt // 2, V_ROWS, 2 * TILE), F32), pltpu.VMEM((1, 2 * TILE), F32)],
        compiler_params=_cparams(("parallel", "parallel")),
        name="attn",
    )(qt, k, vt)


def _mem_kv_kernel(mem_ref, w_ref, mk_ref, mv_ref):
    kv = _dot(mem_ref[...].astype(BF16), w_ref[...])
    mk_ref[...] = kv[:, :MEM_W].astype(BF16)
    mv_ref[...] = kv[:, MEM_W:].astype(BF16)


def _mem_kv(mem, w):
    bsz = mem.shape[0]
    spec = pl.BlockSpec((None, MEM_LEN, MEM_W), lambda b: (b, 0, 0))
    return pl.pallas_call(
        _mem_kv_kernel,
        grid=(bsz,),
        in_specs=[pl.BlockSpec((None, MEM_LEN, D_MODEL), lambda b: (b, 0, 0)),
                  pl.BlockSpec(w.shape, lambda b: (0, 0))],
        out_specs=[spec, spec],
        out_shape=[jax.ShapeDtypeStruct((bsz, MEM_LEN, MEM_W), BF16)] * 2,
        compiler_params=_cparams(("parallel",)),
        name="mem_kv",
    )(mem, w)


def _mem_branch(hb, wq_ref, mk_ref, mv_ref, o_ref):
    scale = MEM_DIM ** -0.5
    mq = _dot(hb, wq_ref[...]).astype(BF16)
    for h in range(MEM_HEADS):
        sl = slice(MEM_DIM * h, MEM_DIM * (h + 1))
        s = _nt(mq[:, sl], mk_ref[:, sl]) * scale
        e = jnp.exp(s - jnp.max(s, axis=-1, keepdims=True))
        o = _dot(e.astype(BF16), mv_ref[:, sl]) / jnp.sum(e, axis=-1, keepdims=True)
        o_ref[:, sl] = o.astype(BF16)


def _mixer_prep_kernel(*refs):
    n_mla, n_fox = len(MLA_WEIGHT_NAMES), len(FOX_WEIGHT_NAMES)
    refs = list(refs)
    h_ref, wc_ref = refs.pop(0), refs.pop(0)
    mla_in = [refs.pop(0) for _ in range(n_mla + 2)]
    fox_in = [refs.pop(0) for _ in range(n_fox)]
    mem_in = [refs.pop(0) for _ in range(3)]
    mla_out = [refs.pop(0) for _ in range(3)]
    fox_out = [refs.pop(0) for _ in range(3)]
    omem_ref, carry_ref = refs
    hb = h_ref[...].astype(BF16)
    c = _dot(hb, wc_ref[...])
    _mla_operands(hb, c, *mla_in, *mla_out)
    _fox_operands(hb, c[:, MLA_Q_LORA + MLA_KV_LORA:], *fox_in, *fox_out, carry_ref)
    _mem_branch(hb, *mem_in, omem_ref)


def _mixer_prep(h, wc, mla_w, cos_t, sin_t, fox_w, wq, mk, mv):
    bsz, seq, _ = h.shape
    op_specs, op_shapes = _attn_operand_specs(bsz, seq)
    tab_spec = pl.BlockSpec((None, MLA_ROPE, TILE), lambda b, s: (b, 0, s))
    kv_spec = pl.BlockSpec((None, MEM_LEN, MEM_W), lambda b, s: (b, 0, 0))
    mla_args = [mla_w[n] for n in MLA_WEIGHT_NAMES]
    fox_args = [fox_w[n] for n in FOX_WEIGHT_NAMES]
    outs = pl.pallas_call(
        _mixer_prep_kernel,
        grid=(bsz, seq // TILE),
        in_specs=[pl.BlockSpec((None, TILE, D_MODEL), lambda b, s: (b, s, 0)), _full(wc.shape)]
        + [_full(a.shape) for a in mla_args] + [tab_spec, tab_spec]
        + [_full(a.shape) for a in fox_args] + [_full(wq.shape), kv_spec, kv_spec],
        out_specs=op_specs + op_specs + [pl.BlockSpec((None, TILE, MEM_W), lambda b, s: (b, s, 0))],
        out_shape=op_shapes + op_shapes + [jax.ShapeDtypeStruct((bsz, seq, MEM_W), BF16)],
        scratch_shapes=[pltpu.VMEM((1, 128), F32)],
        compiler_params=_cparams(("parallel", "arbitrary")),
        name="mixer_prep",
    )(h, wc, *mla_args, cos_t, sin_t, *fox_args, wq, mk, mv)
    return outs[0:3], outs[3:6], outs[6]


def _merge_kernel(h_ref, omla_ref, ofox_ref, omem_ref, wg_ref, bg_ref, wbm_ref, wbf_ref, wbc_ref,
                  wo_ref, g_ref, b_ref, o_ref):
    rows = MERGE_TILE // MERGE_ROW_GROUPS
    for r in range(MERGE_ROW_GROUPS):
        sl = slice(r * rows, (r + 1) * rows)
        tile, off = divmod(r * rows, TILE)
        lanes = slice(off, off + rows)
        h = h_ref[sl, :]
        gates = jax.nn.sigmoid(_dot(h.astype(BF16), wg_ref[...]) + bg_ref[...])
        a_mla = _tn(omla_ref[:, tile].reshape(MLA_W, TILE)[:, lanes], wbm_ref[...])
        a_fox = _tn(ofox_ref[:, tile].reshape(FOX_W, TILE)[:, lanes], wbf_ref[...])
        a_mem = _dot(omem_ref[sl, :], wbc_ref[...])
        merged = (gates[:, :D_MODEL] * a_mla + gates[:, D_MODEL:2 * D_MODEL] * a_fox
                  + gates[:, 2 * D_MODEL:] * a_mem)
        mix = _dot(merged.astype(BF16), wo_ref[...])
        o_ref[sl, :] = _layer_norm(ALPHA * h + mix, g_ref[...], b_ref[...])


def _merge(h, o_mla, o_fox, o_mem, w):
    bsz, seq, _ = h.shape
    ot_spec = pl.BlockSpec((None, HEADS, MERGE_TILE // TILE, 64, TILE), lambda b, s: (b, 0, s, 0, 0))
    names = ["wg", "bg", "wbm", "wbf", "wbc", "wo", "g", "b"]
    return pl.pallas_call(
        _merge_kernel,
        grid=(bsz, seq // MERGE_TILE),
        in_specs=[pl.BlockSpec((None, MERGE_TILE, D_MODEL), lambda b, s: (b, s, 0)), ot_spec, ot_spec,
                  pl.BlockSpec((None, MERGE_TILE, MEM_W), lambda b, s: (b, s, 0))]
        + [_full(w[n].shape) for n in names],
        out_specs=pl.BlockSpec((None, MERGE_TILE, D_MODEL), lambda b, s: (b, s, 0)),
        out_shape=jax.ShapeDtypeStruct((bsz, seq, D_MODEL), F32),
        compiler_params=_cparams(("parallel", "parallel")),
        name="merge",
    )(h, o_mla, o_fox, o_mem, *[w[n] for n in names])


def _placement_matrices():
    pk = np.zeros((128, HEADS * 64), np.float32)
    pq = np.zeros((HEADS * 16, 128), np.float32)
    for h in range(HEADS):
        base = 128 * (h // 2) + (64 if h % 2 == 0 else 0)
        for part in range(3):
            pk[8 * part + h, base + part] = -1.0
            pk[24, base + 3 + part] = 1.0
            pq[16 * h + part, 24] = 1.0
            pq[16 * h + 3 + part, 8 * part + h] = 1.0
    tri = np.tril(np.ones((TILE, TILE), np.float32))
    return jnp.asarray(pk, BF16), jnp.asarray(pq, BF16), jnp.asarray(tri, BF16)


def _swap_halves(w, axis):
    a, b = jnp.split(w, 2, axis=axis)
    return jnp.concatenate([b, a], axis=axis)


def kernel(x, mem, positions, ln1_g, ln1_b, ffn1_w_in, ffn1_w_down, w_in, b_gate, mla_q_norm, mla_w_uq, mla_kv_norm, mla_w_ukv, fox_b_f, mem_w_kv, w_br_mla, w_br_fox, w_br_mem, w_out, ln2_g, ln2_b, ffn2_w_in, ffn2_w_down, ln3_g, ln3_b):
    bsz, seq, _ = x.shape
    n = bsz * seq
    row = lambda v: v.reshape(1, -1).astype(F32)
    bf = lambda v: v.astype(BF16)

    sizes = (MLA_Q_LORA, MLA_KV_LORA, MLA_ROPE, FOX_W, FOX_W, FOX_W, FOX_HEADS, MEM_W)
    offs = np.cumsum((0,) + sizes).tolist()
    wi = w_in[0]
    w_cq, w_ckv, w_kpe, w_fq, w_fk, w_fv, w_fl, w_mq = (wi[:, offs[i]:offs[i + 1]] for i in range(8))
    w_gates = wi[:, offs[8]:]

    uq = mla_w_uq[0].reshape(MLA_Q_LORA, MLA_HEADS, MLA_NOPE + MLA_ROPE)
    uq_rope_sw = _swap_halves(uq[:, :, MLA_NOPE:], axis=2).reshape(MLA_Q_LORA, MLA_HEADS * MLA_ROPE)
    ukv = mla_w_ukv[0].reshape(MLA_KV_LORA, MLA_HEADS, MLA_NOPE + MLA_V)
    w_kpe_t = jnp.concatenate([w_kpe.T, _swap_halves(w_kpe, axis=1).T], axis=0)
    mla_w = {
        "qg": row(mla_q_norm[0]), "kvg": row(mla_kv_norm[0]),
        "wuq": bf(mla_w_uq[0].T), "wuqsw": bf(uq_rope_sw.T), "wkpe": bf(w_kpe_t),
        "wk": bf(ukv[:, :, :MLA_NOPE].reshape(MLA_KV_LORA, -1)),
        "wv": bf(ukv[:, :, MLA_NOPE:].reshape(MLA_KV_LORA, -1).T),
    }

    pk, pq, tri = _placement_matrices()
    w_fl3 = jnp.concatenate([w_fl, w_fl, w_fl, jnp.zeros((D_MODEL, 128 - 3 * FOX_HEADS), F32)], axis=1)
    b_fl3 = jnp.concatenate([fox_b_f[0]] * 3 + [jnp.zeros((128 - 3 * FOX_HEADS,), F32)]).reshape(1, 128)
    w_c = bf(jnp.concatenate([w_cq, w_ckv, w_fl3], axis=1))
    fox_w = {
        "wk": bf(w_fk), "bfl": b_fl3,
        "wq": bf(w_fq.T), "wv": bf(w_fv.T), "tri": tri, "pk": pk, "pq": pq,
    }

    merge_w = {
        "wg": bf(w_gates), "bg": row(b_gate[0]), "wbm": bf(w_br_mla[0]), "wbf": bf(w_br_fox[0]),
        "wbc": bf(w_br_mem[0]), "wo": bf(w_out[0]), "g": row(ln2_g[0]), "b": row(ln2_b[0]),
    }

    h1 = _ffn_ln(x.reshape(n, D_MODEL), bf(ffn1_w_in[0]), bf(ffn1_w_down[0]), row(ln1_g[0]), row(ln1_b[0]))
    h1 = h1.reshape(bsz, seq, D_MODEL)

    cos_t, sin_t = _rope_tables(positions)
    mk, mv = _mem_kv(mem, bf(mem_w_kv[0]))
    mla_ops, fox_ops, o_mem = _mixer_prep(h1, w_c, mla_w, cos_t, sin_t, fox_w, bf(w_mq), mk, mv)
    o_mla = _attention(*mla_ops)
    o_fox = _attention(*fox_ops)

    h2 = _merge(h1, o_mla, o_fox, o_mem, merge_w)
    out = _ffn_ln(h2.reshape(n, D_MODEL), bf(ffn2_w_in[0]), bf(ffn2_w_down[0]), row(ln3_g[0]), row(ln3_b[0]))
    return out.reshape(bsz, seq, D_MODEL)
```

```python
import functools

import numpy as np
import jax
import jax.numpy as jnp
from jax import lax
from jax.experimental import pallas as pl
from jax.experimental.pallas import tpu as pltpu

F32 = jnp.float32
BF16 = jnp.bfloat16

D_MODEL = 1024
MEM_LEN = 256
MLA_HEADS = 8
MLA_NOPE = 64
MLA_ROPE = 32
MLA_V = 64
MLA_Q_LORA = 384
MLA_KV_LORA = 256
FOX_HEADS = 8
FOX_DIM = 64
MEM_HEADS = 4
MEM_DIM = 128
D_FF = 2816
ROPE_THETA = 10000.0
LN_EPS = 1e-5
RMS_EPS = 1e-6
DEPTH = 1
ALPHA = (2.0 * DEPTH) ** 0.25
MLA_W = MLA_HEADS * MLA_V
FOX_W = FOX_HEADS * FOX_DIM
MEM_W = MEM_HEADS * MEM_DIM

HEADS = 8
QK_PAD = 128
V_ROWS = 80
TILE = 512
FF_TILE = 1024
FF_ROW_GROUPS = 4
MIX_TILE = 1024
MIX_ROWS = 512
MERGE_TILE = 1024
MERGE_ROW_GROUPS = 4
VMEM_LIMIT = 56 * 1024 * 1024
LOG2E = 1.4426950408889634


def _cparams(sem):
    return pltpu.CompilerParams(dimension_semantics=sem, vmem_limit_bytes=VMEM_LIMIT)


def _nt(a, b):
    return lax.dot_general(a, b, (((1,), (1,)), ((), ())), preferred_element_type=F32)


def _tn(a, b):
    return lax.dot_general(a, b, (((0,), (0,)), ((), ())), preferred_element_type=F32)


def _dot(a, b):
    return jnp.dot(a, b, preferred_element_type=F32)


def _layer_norm(y, g, b):
    mu = jnp.mean(y, axis=-1, keepdims=True)
    d = y - mu
    var = jnp.mean(d * d, axis=-1, keepdims=True)
    return d * lax.rsqrt(var + LN_EPS) * g + b


def _rms_norm(x, g):
    return x * lax.rsqrt(jnp.mean(x * x, axis=-1, keepdims=True) + RMS_EPS) * g


def _split3(x):
    hi = x.astype(BF16).astype(F32)
    r = x - hi
    mid = r.astype(BF16).astype(F32)
    lo = (r - mid).astype(BF16).astype(F32)
    return hi, mid, lo


def _ffn_ln_kernel(x_ref, wa_ref, wb_ref, wd_ref, g_ref, b_ref, o_ref):
    rows = FF_TILE // FF_ROW_GROUPS
    for r in range(FF_ROW_GROUPS):
        sl = slice(r * rows, (r + 1) * rows)
        x = x_ref[sl, :]
        xb = x.astype(BF16)
        a = _dot(xb, wa_ref[...])
        b = _dot(xb, wb_ref[...])
        gated = (a * jax.nn.sigmoid(a) * b).astype(BF16)
        y = ALPHA * x + 0.5 * _dot(gated, wd_ref[...])
        o_ref[sl, :] = _layer_norm(y, g_ref[...], b_ref[...])


def _resident(shape, index_map):
    return pl.BlockSpec(shape, index_map, pipeline_mode=pl.Buffered(1))


def _ffn_ln(x, w_in, w_down, g, b):
    n = x.shape[0]
    return pl.pallas_call(
        _ffn_ln_kernel,
        grid=(n // FF_TILE,),
        in_specs=[
            pl.BlockSpec((FF_TILE, D_MODEL), lambda i: (i, 0)),
            _resident((D_MODEL, D_FF), lambda i: (0, 0)),
            _resident((D_MODEL, D_FF), lambda i: (0, 1)),
            _resident((D_FF, D_MODEL), lambda i: (0, 0)),
            _resident((1, D_MODEL), lambda i: (0, 0)),
            _resident((1, D_MODEL), lambda i: (0, 0)),
        ],
        out_specs=pl.BlockSpec((FF_TILE, D_MODEL), lambda i: (i, 0)),
        out_shape=jax.ShapeDtypeStruct((n, D_MODEL), F32),
        compiler_params=_cparams(("parallel",)),
        name="ffn_ln",
    )(x, w_in, w_in, w_down, g, b)


def _rope_kernel(pos_ref, inv_ref, cos_ref, sin_ref):
    ang = pos_ref[...].astype(F32) * inv_ref[...]
    c = jnp.cos(ang)
    s = jnp.sin(ang)
    cos_ref[...] = jnp.concatenate([c, c], axis=0)
    sin_ref[...] = jnp.concatenate([-s, s], axis=0)


def _rope_tables(positions):
    bsz, seq = positions.shape
    half = MLA_ROPE // 2
    inv_freq = (ROPE_THETA ** (-jnp.arange(half, dtype=F32) / half)).reshape(half, 1)
    ts = min(seq, 2048)
    spec = pl.BlockSpec((None, MLA_ROPE, ts), lambda b, s: (b, 0, s))
    return pl.pallas_call(
        _rope_kernel,
        grid=(bsz, seq // ts),
        in_specs=[pl.BlockSpec((None, 1, ts), lambda b, s: (b, 0, s)),
                  pl.BlockSpec((half, 1), lambda b, s: (0, 0))],
        out_specs=[spec, spec],
        out_shape=[jax.ShapeDtypeStruct((bsz, MLA_ROPE, seq), F32)] * 2,
        compiler_params=_cparams(("parallel", "parallel")),
        name="rope",
    )(positions.reshape(bsz, 1, seq), inv_freq)


def _attn_operand_specs(bsz, seq):
    nt = seq // TILE
    per_step = MIX_TILE // TILE
    qt_spec = pl.BlockSpec((None, HEADS, per_step, QK_PAD, TILE), lambda b, s: (b, 0, s, 0, 0))
    k_spec = pl.BlockSpec((None, MIX_TILE, HEADS * QK_PAD), lambda b, s: (b, s, 0))
    vt_spec = pl.BlockSpec((None, HEADS, per_step, V_ROWS, TILE), lambda b, s: (b, 0, s, 0, 0))
    shapes = [jax.ShapeDtypeStruct((bsz, HEADS, nt, QK_PAD, TILE), BF16),
              jax.ShapeDtypeStruct((bsz, seq, HEADS * QK_PAD), BF16),
              jax.ShapeDtypeStruct((bsz, HEADS, nt, V_ROWS, TILE), BF16)]
    return [qt_spec, k_spec, vt_spec], shapes


def _store_vt(vt_ref, vt):
    row = lax.broadcasted_iota(jnp.int32, (V_ROWS - 64, MIX_ROWS), 0)
    tail = jnp.where(row == 0, 1.0, 0.0).astype(BF16)
    for h in range(HEADS):
        vt_ref[h, 0:64, :] = vt[64 * h:64 * h + 64].astype(BF16)
        vt_ref[h, 64:V_ROWS, :] = tail


def _full(shape):
    return pl.BlockSpec(shape, lambda b, s: (0,) * len(shape))


def _mla_operands(hb, c, qg_ref, kvg_ref, wuq_ref, wuqsw_ref, wkpe_ref, wk_ref, wv_ref,
                  cos_ref, sin_ref, qt_ref, k_ref, vt_ref):
    scale = (MLA_NOPE + MLA_ROPE) ** -0.5 * LOG2E
    cqn = _rms_norm(c[:, :MLA_Q_LORA], qg_ref[...]).astype(BF16)
    kvn = _rms_norm(c[:, MLA_Q_LORA:MLA_Q_LORA + MLA_KV_LORA], kvg_ref[...]).astype(BF16)
    cos_t = cos_ref[...]
    sin_t = sin_ref[...]

    q_t = _nt(wuq_ref[...], cqn)
    qsw_t = _nt(wuqsw_ref[...], cqn)
    dq = MLA_NOPE + MLA_ROPE
    pad = jnp.zeros((QK_PAD - dq, MIX_ROWS), BF16)
    for h in range(HEADS):
        nope = (q_t[dq * h:dq * h + MLA_NOPE] * scale).astype(BF16)
        rope = (q_t[dq * h + MLA_NOPE:dq * (h + 1)] * cos_t
                + qsw_t[MLA_ROPE * h:MLA_ROPE * (h + 1)] * sin_t)
        rope = (rope * scale).astype(BF16)
        if h % 2 == 0:
            qt_ref[h, 0:MLA_NOPE, :] = nope
            qt_ref[h, MLA_NOPE:dq, :] = rope
            qt_ref[h, dq:QK_PAD, :] = pad
        else:
            qt_ref[h, 0:MLA_ROPE, :] = rope
            qt_ref[h, MLA_ROPE:QK_PAD - MLA_NOPE, :] = pad
            qt_ref[h, QK_PAD - MLA_NOPE:QK_PAD, :] = nope

    kpe_t = _nt(wkpe_ref[...], hb)
    kr_t = kpe_t[0:MLA_ROPE] * cos_t + kpe_t[MLA_ROPE:2 * MLA_ROPE] * sin_t
    gap = jnp.zeros((MLA_NOPE - MLA_ROPE, MIX_ROWS), F32)
    kr = jnp.concatenate([kr_t, gap, kr_t, gap], axis=0).T
    k_nope = _dot(kvn, wk_ref[...])
    lane = lax.broadcasted_iota(jnp.int32, (MIX_ROWS, 128), 1)
    for g in range(HEADS // 2):
        pair = k_nope[:, 128 * g:128 * (g + 1)]
        k_ref[:, 256 * g:256 * g + 128] = jnp.where(lane < 64, pair, kr).astype(BF16)
        k_ref[:, 256 * g + 128:256 * (g + 1)] = jnp.where(lane < 64, kr, pair).astype(BF16)

    _store_vt(vt_ref, _nt(wv_ref[...], kvn))


MLA_WEIGHT_NAMES = ("qg", "kvg", "wuq", "wuqsw", "wkpe", "wk", "wv")
FOX_WEIGHT_NAMES = ("wk", "bfl", "wq", "wv", "tri", "pk", "pq")


def _fox_operands(hb, z, wk_ref, bfl_ref, wq_ref, wv_ref, tri_ref, pk_ref, pq_ref,
                  qt_ref, k_ref, vt_ref, carry_ref):
    lane = lax.broadcasted_iota(jnp.int32, (MIX_ROWS, 128), 1)
    z = z + bfl_ref[...]
    log_f = jnp.minimum(z, 0.0) - jnp.log1p(jnp.exp(-jnp.abs(z)))
    hi, mid, lo = _split3(log_f)
    parts = jnp.where(lane < 8, hi, jnp.where(lane < 16, mid, jnp.where(lane < 24, lo, 0.0)))
    cum = _dot(tri_ref[...], parts.astype(BF16))
    f = cum + pltpu.roll(cum, 120, 1) + pltpu.roll(cum, 112, 1) + carry_ref[...]
    f = jnp.where(lane < 8, f, 0.0)
    carry_ref[...] = f[MIX_ROWS - 1:MIX_ROWS, :]

    fhi, fmid, flo = _split3(f * LOG2E)
    p = (fhi + pltpu.roll(fmid, 8, 1) + pltpu.roll(flo, 16, 1)
         + jnp.where(lane == 24, 1.0, 0.0)).astype(BF16)
    fk = _dot(hb, wk_ref[...])
    aug = _dot(p, pk_ref[...])
    aug_t = _nt(pq_ref[...], p)
    q_t = _nt(wq_ref[...], hb)
    scale = FOX_DIM ** -0.5 * LOG2E
    for g in range(HEADS // 2):
        fk_pair = fk[:, 128 * g:128 * (g + 1)]
        aug_pair = aug[:, 128 * g:128 * (g + 1)]
        k_ref[:, 256 * g:256 * g + 128] = jnp.where(lane < 64, fk_pair, aug_pair).astype(BF16)
        k_ref[:, 256 * g + 128:256 * (g + 1)] = jnp.where(lane < 64, aug_pair, fk_pair).astype(BF16)
        he, ho = 2 * g, 2 * g + 1
        qt_ref[he, 0:64, :] = (q_t[64 * he:64 * he + 64] * scale).astype(BF16)
        qt_ref[he, 64:80, :] = aug_t[16 * he:16 * he + 16].astype(BF16)
        qt_ref[he, 80:QK_PAD, :] = jnp.zeros((QK_PAD - 80, MIX_ROWS), BF16)
        qt_ref[ho, 0:16, :] = aug_t[16 * ho:16 * ho + 16].astype(BF16)
        qt_ref[ho, 16:64, :] = jnp.zeros((48, MIX_ROWS), BF16)
        qt_ref[ho, 64:QK_PAD, :] = (q_t[64 * ho:64 * ho + 64] * scale).astype(BF16)

    _store_vt(vt_ref, _nt(wv_ref[...], hb))


MAX_LAG = 16.0
MAX_OVERSHOOT = 96.0
DIAG_TILES_PER_STEP = 4
FAST_CHUNKS_PER_STEP = 14
FAST_PIECE = 256
FAST_LOOKAHEAD = 3


def _attn_kernel(qt_ref, k_ref, vt_ref, o_ref, s0_ref, s1_ref, s2_ref, s3_ref, cm_ref, m_ref, acc_ref,
                 flag_ref, *, n_tiles):
    t = TILE
    nq = n_tiles // 2
    row = lax.broadcasted_iota(jnp.int32, (t, t), 0)
    col = lax.broadcasted_iota(jnp.int32, (t, t), 1)

    def logits(kblk, qtile, half, s_ref, slot):
        k = k_ref[pl.ds(pl.multiple_of(kblk * t, t), t), :]
        s = _dot(k, qt_ref[2 * qtile + half])
        s_ref[:, half * t:(half + 1) * t] = s
        cm_ref[slot, :, half * t:(half + 1) * t] = jnp.max(s, axis=0, keepdims=True)

    def softmax_pv(kblk, qtile, half, s_ref, slot, diagonal=False, first=False, between=None):
        cols = slice(half * t, (half + 1) * t)
        s = s_ref[:, cols]
        if diagonal:
            s = jnp.where(row > col, -jnp.inf, s)
            cm = jnp.max(s, axis=0, keepdims=True)
        else:
            cm = cm_ref[slot, :, cols]
        m_old = cm if first else m_ref[qtile, :, cols]
        m_new = cm if first else jnp.maximum(m_old, cm)
        m_ref[qtile, :, cols] = m_new
        p = jnp.exp2(s - m_new).astype(BF16)
        if between is None:
            pv = _dot(vt_ref[kblk], p)
        else:
            pv = None
            for piece in range(2):
                keys = slice(piece * (t // 2), (piece + 1) * (t // 2))
                d = _dot(vt_ref[kblk, :, keys], p[keys])
                pv = d if pv is None else pv + d
                between()
        if first:
            acc_ref[qtile, :, cols] = pv
        else:
            acc_ref[qtile, :, cols] = jnp.exp2(m_old - m_new) * acc_ref[qtile, :, cols] + pv

    def diag_logits_pieces(q, sa_ref, sb_ref, slot):
        units = ((2 * q, 0, sa_ref, slot), (2 * q, 1, sa_ref, slot), (2 * q + 1, 1, sb_ref, slot + 1))

        def piece(kblk, half, s_ref, slot, part):
            cols = slice(half * t, (half + 1) * t)
            keys = slice(part * (t // 2), (part + 1) * (t // 2))
            start = pl.multiple_of(kblk * t + part * (t // 2), t // 2)
            s = _dot(k_ref[pl.ds(start, t // 2), :], qt_ref[2 * q + half])
            s_ref[keys, cols] = s
            pm = jnp.max(s, axis=0, keepdims=True)
            cm_ref[slot, :, cols] = pm if part == 0 else jnp.maximum(cm_ref[slot, :, cols], pm)

        return [functools.partial(piece, *u, part) for u in units for part in range(2)]

    def diagonal_phase():
        n = DIAG_TILES_PER_STEP if nq % DIAG_TILES_PER_STEP == 0 else 2
        bufs = ((s0_ref, s1_ref, 0), (s2_ref, s3_ref, 2))
        for thunk in diag_logits_pieces(0, *bufs[0]):
            thunk()

        def step(i, c):
            for r in range(n):
                cur = n * i + r
                pending = diag_logits_pieces(jnp.minimum(cur + 1, nq - 1), *bufs[(r + 1) % 2])

                def queue_next():
                    if pending:
                        pending.pop(0)()

                sa_ref, sb_ref, slot = bufs[r % 2]
                queue_next()
                queue_next()
                softmax_pv(2 * cur, cur, 0, sa_ref, slot, diagonal=True, first=True, between=queue_next)
                softmax_pv(2 * cur, cur, 1, sa_ref, slot, first=True, between=queue_next)
                softmax_pv(2 * cur + 1, cur, 1, sb_ref, slot + 1, diagonal=True, between=queue_next)
                while pending:
                    queue_next()
            return c

        lax.fori_loop(0, nq // n, step, 0)

    def advance(q, j):
        wrap = j + 1 >= q
        return jnp.where(wrap, jnp.minimum(q + 1, nq - 1), q), jnp.where(wrap, 0, j + 1)

    n_chunks = nq * (nq - 1) // 2
    first_chunk = (jnp.int32(1), jnp.int32(0))

    pieces_per_unit = 2 * t // FAST_PIECE

    def piece_logits(unit, piece):
        q, j, half = unit
        start = pl.multiple_of(j * 2 * t + piece * FAST_PIECE, FAST_PIECE)
        return _dot(k_ref[pl.ds(start, FAST_PIECE), :], qt_ref[2 * q + half])

    def fast_units(units):
        seq = [(u, piece) for u in units for piece in range(pieces_per_unit)]
        ahead = [piece_logits(*seq[i]) for i in range(min(FAST_LOOKAHEAD, len(seq)))]
        for i, (unit, piece) in enumerate(seq):
            q, j, half = unit
            cols = slice(half * t, (half + 1) * t)
            s = ahead.pop(0)
            if i + FAST_LOOKAHEAD < len(seq):
                ahead.append(piece_logits(*seq[i + FAST_LOOKAHEAD]))
            if piece == 0:
                m_old = m_ref[q, :, cols]
                cm, pv = None, None
            pm = jnp.max(s, axis=0, keepdims=True)
            cm = pm if cm is None else jnp.maximum(cm, pm)
            p = jnp.exp2(s - m_old).astype(BF16)
            blk, off = divmod(piece * FAST_PIECE, t)
            d = _dot(vt_ref[2 * j + blk, :, off:off + FAST_PIECE], p)
            pv = d if pv is None else pv + d
            if piece == pieces_per_unit - 1:
                m_new = jnp.maximum(m_old, cm - MAX_LAG)
                acc_ref[q, :, cols] = (acc_ref[q, :, cols] + pv) * jnp.exp2(m_old - m_new)
                m_ref[q, :, cols] = m_new
                flag_ref[:, cols] = jnp.maximum(flag_ref[:, cols], cm - m_old)

    def fast_phase():
        def chunks_from(c, n):
            units = []
            for _ in range(n):
                units += [(c[0], c[1], 0), (c[0], c[1], 1)]
                c = advance(*c)
            return units, c

        def step(i, c):
            units, c = chunks_from(c, FAST_CHUNKS_PER_STEP)
            fast_units(units)
            return c

        c = lax.fori_loop(0, n_chunks // FAST_CHUNKS_PER_STEP, step, first_chunk)
        if n_chunks % FAST_CHUNKS_PER_STEP:
            fast_units(chunks_from(c, n_chunks % FAST_CHUNKS_PER_STEP)[0])

    def safe_chunk(q, j, qn, jn):
        for half in range(2):
            logits(2 * j + 1, q, half, s1_ref, 1)
        for half in range(2):
            softmax_pv(2 * j, q, half, s0_ref, 0)
        for half in range(2):
            logits(2 * jn, qn, half, s0_ref, 0)
        for half in range(2):
            softmax_pv(2 * j + 1, q, half, s1_ref, 1)

    def safe_phase():
        for half in range(2):
            logits(0, 1, half, s0_ref, 0)

        def step(i, c):
            nxt = advance(*c)
            safe_chunk(*c, *nxt)
            return nxt

        lax.fori_loop(0, n_chunks, step, first_chunk)

    def finalize():
        def body(q, c):
            for half in range(2):
                cols = slice(half * t, (half + 1) * t)
                o_ref[2 * q + half] = (acc_ref[q, 0:64, cols] / acc_ref[q, 64:65, cols]).astype(BF16)
            return c

        lax.fori_loop(0, nq, body, 0)

    flag_ref[...] = jnp.full(flag_ref.shape, -jnp.inf, F32)
    diagonal_phase()
    if n_chunks:
        fast_phase()
    finalize()

    @pl.when(jnp.max(flag_ref[...]) > MAX_OVERSHOOT)
    def _():
        diagonal_phase()
        if n_chunks:
            safe_phase()
        finalize()


def _attention(qt, k, vt):
    bsz, heads, nt = qt.shape[:3]
    seq = k.shape[1]
    return pl.pallas_call(
        functools.partial(_attn_kernel, n_tiles=nt),
        grid=(bsz, heads),
        in_specs=[pl.BlockSpec((None, None, nt, QK_PAD, TILE), lambda b, h: (b, h, 0, 0, 0)),
                  pl.BlockSpec((None, seq, QK_PAD), lambda b, h: (b, 0, h)),
                  pl.BlockSpec((None, None, nt, V_ROWS, TILE), lambda b, h: (b, h, 0, 0, 0))],
        out_specs=pl.BlockSpec((None, None, nt, 64, TILE), lambda b, h: (b, h, 0, 0, 0)),
        out_shape=jax.ShapeDtypeStruct((bsz, heads, nt, 64, TILE), BF16),
        scratch_shapes=[pltpu.VMEM((TILE, 2 * TILE), F32)] * 4
        + [pltpu.VMEM((4, 1, 2 * TILE), F32), pltpu.VMEM((nt // 2, 1, 2 * TILE), F32),
           pltpu.VMEM((nt // 2, V_ROWS, 2 * TILE), F32), pltpu.VMEM((1, 2 * TILE), F32)],
        compiler_params=_cparams(("parallel", "parallel")),
        name="attn",
    )(qt, k, vt)


def _mem_kv_kernel(mem_ref, w_ref, mk_ref, mv_ref):
    kv = _dot(mem_ref[...].astype(BF16), w_ref[...])
    mk_ref[...] = kv[:, :MEM_W].astype(BF16)
    mv_ref[...] = kv[:, MEM_W:].astype(BF16)


def _mem_kv(mem, w):
    bsz = mem.shape[0]
    spec = pl.BlockSpec((None, MEM_LEN, MEM_W), lambda b: (b, 0, 0))
    return pl.pallas_call(
        _mem_kv_kernel,
        grid=(bsz,),
        in_specs=[pl.BlockSpec((None, MEM_LEN, D_MODEL), lambda b: (b, 0, 0)),
                  pl.BlockSpec(w.shape, lambda b: (0, 0))],
        out_specs=[spec, spec],
        out_shape=[jax.ShapeDtypeStruct((bsz, MEM_LEN, MEM_W), BF16)] * 2,
        compiler_params=_cparams(("parallel",)),
        name="mem_kv",
    )(mem, w)


def _mem_branch(hb, wq_ref, mk_ref, mv_ref, o_ref):
    scale = MEM_DIM ** -0.5
    mq = _dot(hb, wq_ref[...]).astype(BF16)
    for h in range(MEM_HEADS):
        sl = slice(MEM_DIM * h, MEM_DIM * (h + 1))
        s = _nt(mq[:, sl], mk_ref[:, sl]) * scale
        e = jnp.exp(s - jnp.max(s, axis=-1, keepdims=True))
        o = _dot(e.astype(BF16), mv_ref[:, sl]) / jnp.sum(e, axis=-1, keepdims=True)
        o_ref[:, sl] = o.astype(BF16)


def _mixer_prep_kernel(*refs):
    n_mla, n_fox = len(MLA_WEIGHT_NAMES), len(FOX_WEIGHT_NAMES)
    refs = list(refs)
    h_ref, wc_ref = refs.pop(0), refs.pop(0)
    mla_in = [refs.pop(0) for _ in range(n_mla + 2)]
    fox_in = [refs.pop(0) for _ in range(n_fox)]
    mem_in = [refs.pop(0) for _ in range(3)]
    mla_out = [refs.pop(0) for _ in range(3)]
    fox_out = [refs.pop(0) for _ in range(3)]
    omem_ref, carry_ref = refs

    @pl.when(pl.program_id(1) == 0)
    def _():
        carry_ref[...] = jnp.zeros_like(carry_ref)

    def operand_views(out, g, tok):
        return [out[0].at[:, g], out[1].at[tok, :], out[2].at[:, g]]

    for g in range(MIX_TILE // TILE):
        tok = pl.ds(g * TILE, TILE)
        hb = h_ref[tok, :].astype(BF16)
        c = _dot(hb, wc_ref[...])
        tabs = [ref.at[:, tok] for ref in mla_in[n_mla:]]
        _mla_operands(hb, c, *mla_in[:n_mla], *tabs, *operand_views(mla_out, g, tok))
        _fox_operands(hb, c[:, MLA_Q_LORA + MLA_KV_LORA:], *fox_in, *operand_views(fox_out, g, tok), carry_ref)
        _mem_branch(hb, *mem_in, omem_ref.at[tok, :])


def _mixer_prep(h, wc, mla_w, cos_t, sin_t, fox_w, wq, mk, mv):
    bsz, seq, _ = h.shape
    op_specs, op_shapes = _attn_operand_specs(bsz, seq)
    tab_spec = pl.BlockSpec((None, MLA_ROPE, MIX_TILE), lambda b, s: (b, 0, s))
    kv_spec = pl.BlockSpec((None, MEM_LEN, MEM_W), lambda b, s: (b, 0, 0))
    mla_args = [mla_w[n] for n in MLA_WEIGHT_NAMES]
    fox_args = [fox_w[n] for n in FOX_WEIGHT_NAMES]
    weight = lambda a: _resident(a.shape, lambda b, s: (0,) * a.ndim)
    outs = pl.pallas_call(
        _mixer_prep_kernel,
        grid=(bsz, seq // MIX_TILE),
        in_specs=[pl.BlockSpec((None, MIX_TILE, D_MODEL), lambda b, s: (b, s, 0)), weight(wc)]
        + [weight(a) for a in mla_args] + [tab_spec, tab_spec]
        + [weight(a) for a in fox_args] + [weight(wq), kv_spec, kv_spec],
        out_specs=op_specs + op_specs + [pl.BlockSpec((None, MIX_TILE, MEM_W), lambda b, s: (b, s, 0))],
        out_shape=op_shapes + op_shapes + [jax.ShapeDtypeStruct((bsz, seq, MEM_W), BF16)],
        scratch_shapes=[pltpu.VMEM((1, 128), F32)],
        compiler_params=_cparams(("parallel", "arbitrary")),
        name="mixer_prep",
    )(h, wc, *mla_args, cos_t, sin_t, *fox_args, wq, mk, mv)
    return outs[0:3], outs[3:6], outs[6]


def _merge_kernel(h_ref, omla_ref, ofox_ref, omem_ref, wg_ref, bg_ref, wbm_ref, wbf_ref, wbc_ref,
                  wo_ref, g_ref, b_ref, o_ref):
    rows = MERGE_TILE // MERGE_ROW_GROUPS
    for r in range(MERGE_ROW_GROUPS):
        sl = slice(r * rows, (r + 1) * rows)
        tile, off = divmod(r * rows, TILE)
        lanes = slice(off, off + rows)
        h = h_ref[sl, :]
        gates = jax.nn.sigmoid(_dot(h.astype(BF16), wg_ref[...]) + bg_ref[...])
        a_mla = _tn(omla_ref[:, tile].reshape(MLA_W, TILE)[:, lanes], wbm_ref[...])
        a_fox = _tn(ofox_ref[:, tile].reshape(FOX_W, TILE)[:, lanes], wbf_ref[...])
        a_mem = _dot(omem_ref[sl, :], wbc_ref[...])
        merged = (gates[:, :D_MODEL] * a_mla + gates[:, D_MODEL:2 * D_MODEL] * a_fox
                  + gates[:, 2 * D_MODEL:] * a_mem)
        mix = _dot(merged.astype(BF16), wo_ref[...])
        o_ref[sl, :] = _layer_norm(ALPHA * h + mix, g_ref[...], b_ref[...])


def _merge(h, o_mla, o_fox, o_mem, w):
    bsz, seq, _ = h.shape
    ot_spec = pl.BlockSpec((None, HEADS, MERGE_TILE // TILE, 64, TILE), lambda b, s: (b, 0, s, 0, 0))
    names = ["wg", "bg", "wbm", "wbf", "wbc", "wo", "g", "b"]
    return pl.pallas_call(
        _merge_kernel,
        grid=(bsz, seq // MERGE_TILE),
        in_specs=[pl.BlockSpec((None, MERGE_TILE, D_MODEL), lambda b, s: (b, s, 0)), ot_spec, ot_spec,
                  pl.BlockSpec((None, MERGE_TILE, MEM_W), lambda b, s: (b, s, 0))]
        + [_full(w[n].shape) for n in names],
        out_specs=pl.BlockSpec((None, MERGE_TILE, D_MODEL), lambda b, s: (b, s, 0)),
        out_shape=jax.ShapeDtypeStruct((bsz, seq, D_MODEL), F32),
        compiler_params=_cparams(("parallel", "parallel")),
        name="merge",
    )(h, o_mla, o_fox, o_mem, *[w[n] for n in names])


def _placement_matrices():
    pk = np.zeros((128, HEADS * 64), np.float32)
    pq = np.zeros((HEADS * 16, 128), np.float32)
    for h in range(HEADS):
        base = 128 * (h // 2) + (64 if h % 2 == 0 else 0)
        for part in range(3):
            pk[8 * part + h, base + part] = -1.0
            pk[24, base + 3 + part] = 1.0
            pq[16 * h + part, 24] = 1.0
            pq[16 * h + 3 + part, 8 * part + h] = 1.0
    tri = np.tril(np.ones((MIX_ROWS, MIX_ROWS), np.float32))
    return jnp.asarray(pk, BF16), jnp.asarray(pq, BF16), jnp.asarray(tri, BF16)


def _swap_halves(w, axis):
    a, b = jnp.split(w, 2, axis=axis)
    return jnp.concatenate([b, a], axis=axis)


def kernel(x, mem, positions, ln1_g, ln1_b, ffn1_w_in, ffn1_w_down, w_in, b_gate, mla_q_norm, mla_w_uq, mla_kv_norm, mla_w_ukv, fox_b_f, mem_w_kv, w_br_mla, w_br_fox, w_br_mem, w_out, ln2_g, ln2_b, ffn2_w_in, ffn2_w_down, ln3_g, ln3_b):
    bsz, seq, _ = x.shape
    n = bsz * seq
    row = lambda v: v.reshape(1, -1).astype(F32)
    bf = lambda v: v.astype(BF16)

    sizes = (MLA_Q_LORA, MLA_KV_LORA, MLA_ROPE, FOX_W, FOX_W, FOX_W, FOX_HEADS, MEM_W)
    offs = np.cumsum((0,) + sizes).tolist()
    wi = w_in.reshape(D_MODEL, -1)
    w_cq, w_ckv, w_kpe, w_fq, w_fk, w_fv, w_fl, w_mq = (wi[:, offs[i]:offs[i + 1]] for i in range(8))
    w_gates = wi[:, offs[8]:]

    uq = mla_w_uq[0].reshape(MLA_Q_LORA, MLA_HEADS, MLA_NOPE + MLA_ROPE)
    uq_rope_sw = _swap_halves(uq[:, :, MLA_NOPE:], axis=2).reshape(MLA_Q_LORA, MLA_HEADS * MLA_ROPE)
    ukv = mla_w_ukv[0].reshape(MLA_KV_LORA, MLA_HEADS, MLA_NOPE + MLA_V)
    w_kpe_t = jnp.concatenate([w_kpe.T, _swap_halves(w_kpe, axis=1).T], axis=0)
    mla_w = {
        "qg": row(mla_q_norm[0]), "kvg": row(mla_kv_norm[0]),
        "wuq": bf(mla_w_uq[0].T), "wuqsw": bf(uq_rope_sw.T), "wkpe": bf(w_kpe_t),
        "wk": bf(ukv[:, :, :MLA_NOPE].reshape(MLA_KV_LORA, -1)),
        "wv": bf(ukv[:, :, MLA_NOPE:].reshape(MLA_KV_LORA, -1).T),
    }

    pk, pq, tri = _placement_matrices()
    w_fl3 = jnp.concatenate([w_fl, w_fl, w_fl, jnp.zeros((D_MODEL, 128 - 3 * FOX_HEADS), F32)], axis=1)
    b_fl3 = jnp.concatenate([fox_b_f[0]] * 3 + [jnp.zeros((128 - 3 * FOX_HEADS,), F32)]).reshape(1, 128)
    w_c = bf(jnp.concatenate([w_cq, w_ckv, w_fl3], axis=1))
    fox_w = {
        "wk": bf(w_fk), "bfl": b_fl3,
        "wq": bf(w_fq.T), "wv": bf(w_fv.T), "tri": tri, "pk": pk, "pq": pq,
    }

    merge_w = {
        "wg": bf(w_gates), "bg": row(b_gate[0]), "wbm": bf(w_br_mla[0]), "wbf": bf(w_br_fox[0]),
        "wbc": bf(w_br_mem[0]), "wo": bf(w_out[0]), "g": row(ln2_g[0]), "b": row(ln2_b[0]),
    }

    h1 = _ffn_ln(x.reshape(n, D_MODEL), bf(ffn1_w_in[0]), bf(ffn1_w_down[0]), row(ln1_g[0]), row(ln1_b[0]))
    h1 = h1.reshape(bsz, seq, D_MODEL)

    cos_t, sin_t = _rope_tables(positions)
    mk, mv = _mem_kv(mem, bf(mem_w_kv[0]))
    mla_ops, fox_ops, o_mem = _mixer_prep(h1, w_c, mla_w, cos_t, sin_t, fox_w, bf(w_mq), mk, mv)
    o_mla = _attention(*mla_ops)
    o_fox = _attention(*fox_ops)

    h2 = _merge(h1, o_mla, o_fox, o_mem, merge_w)
    out = _ffn_ln(h2.reshape(n, D_MODEL), bf(ffn2_w_in[0]), bf(ffn2_w_down[0]), row(ln3_g[0]), row(ln3_b[0]))
    return out.reshape(bsz, seq, D_MODEL)
```
